```python
import math
import jax, jax.numpy as jnp
from jax import lax
import numpy as np

D_MODEL = 1024
BATCH = 16
SEQ = 4096
DEPTH = 1
DEC_BATCH = 16
DEC_SEQ = 2048
PAST_LEN = 128

D_MIX = D_MODEL
ATTN_WIDTH = D_MIX // 2
POOL_WIDTH = D_MIX - ATTN_WIDTH
V_HEAD_DIM = 128
MLA_HEADS = ATTN_WIDTH // V_HEAD_DIM
QK_NOPE_DIM = 128
QK_ROPE_DIM = 64
QK_DIM = QK_NOPE_DIM + QK_ROPE_DIM
Q_LORA_RANK = 384
KV_LORA_RANK = 256
ROPE_THETA = 10000.0
Q_BLOCK = 128
POOL_WINDOWS = (2, 4, 8, 16)
N_POOL_GROUPS = len(POOL_WINDOWS)
POOL_CH = POOL_WIDTH // N_POOL_GROUPS
IN_WIDTH = Q_LORA_RANK + KV_LORA_RANK + QK_ROPE_DIM + POOL_WIDTH
N_GROUPS = 4
EXPERTS_PER_GROUP = 8
N_EXPERTS = N_GROUPS * EXPERTS_PER_GROUP
TOP_K = 2
EXPERT_HIDDEN = 256
EPS = 1e-6

kernel_name = "hymba_mla_pool_hmoe_encoder"


def rms_norm(x, g):
    xf = x.astype(jnp.float32)
    y = xf * lax.rsqrt(jnp.mean(xf * xf, axis=-1, keepdims=True) + EPS)
    return (y * g.astype(jnp.float32)).astype(x.dtype)


def rope_tables(seq):
    inv_freq = 1.0 / (ROPE_THETA ** (jnp.arange(0, QK_ROPE_DIM, 2, dtype=jnp.float32) / QK_ROPE_DIM))
    ang = jnp.arange(seq, dtype=jnp.float32)[:, None] * inv_freq[None, :]
    emb = jnp.concatenate([ang, ang], axis=-1)
    return jnp.cos(emb)[:, None, :], jnp.sin(emb)[:, None, :]


def apply_rope(x, cos, sin):
    xf = x.astype(jnp.float32)
    x1, x2 = jnp.split(xf, 2, axis=-1)
    rot = jnp.concatenate([-x2, x1], axis=-1)
    return (xf * cos + rot * sin).astype(x.dtype)


def bidirectional_attention(q, k, v):
    B, S, H, Dq = q.shape
    nb = S // Q_BLOCK
    qb = q.reshape(B, nb, Q_BLOCK, H, Dq).transpose(1, 0, 2, 3, 4)
    scale = QK_DIM ** -0.5

    def one_block(qblk):
        s = jnp.einsum('bqhd,bkhd->bhqk', qblk, k).astype(jnp.float32) * scale
        p = jax.nn.softmax(s, axis=-1)
        return jnp.einsum('bhqk,bkhd->bqhd', p.astype(v.dtype), v)

    o = lax.map(one_block, qb)
    return o.transpose(1, 0, 2, 3, 4).reshape(B, S, H * V_HEAD_DIM)


def multi_scale_pool(u, w_pool, pool_scale):
    B, S, _ = u.shape
    ug = u.reshape(B, S, N_POOL_GROUPS, POOL_CH)
    uf = ug.astype(jnp.float32)
    c = jnp.cumsum(uf, axis=1)
    c = jnp.concatenate([jnp.zeros((B, 1, N_POOL_GROUPS, POOL_CH), jnp.float32), c], axis=1)
    wins = np.array(POOL_WINDOWS, dtype=np.int32)
    lo = jnp.asarray(wins // 2)
    hi = jnp.asarray(wins - wins // 2 - 1)
    t = jnp.arange(S, dtype=jnp.int32)
    idx_hi = jnp.clip(t[:, None] + hi[None, :] + 1, 0, S)
    idx_lo = jnp.clip(t[:, None] - lo[None, :], 0, S)
    s_hi = jnp.take_along_axis(c, idx_hi[None, :, :, None], axis=1)
    s_lo = jnp.take_along_axis(c, idx_lo[None, :, :, None], axis=1)
    count = (idx_hi - idx_lo).astype(jnp.float32)[None, :, :, None]
    d = ((s_hi - s_lo) / count - uf).astype(u.dtype)
    mixed = jnp.einsum('bsgc,gcd->bsgd', d, w_pool).reshape(B, S, POOL_WIDTH)
    return mixed * pool_scale


def token_mixer(a, w_in, g_q_lat, g_kv_lat, w_uq, w_ukv, g_q_head, g_k_head, w_pool, pool_scale, w_out):
    B, S, _ = a.shape
    p = a @ w_in
    c_q, c_kv, k_r, u = jnp.split(
        p, [Q_LORA_RANK, Q_LORA_RANK + KV_LORA_RANK, Q_LORA_RANK + KV_LORA_RANK + QK_ROPE_DIM], axis=-1)
    c_q = rms_norm(c_q, g_q_lat)
    c_kv = rms_norm(c_kv, g_kv_lat)
    q = (c_q @ w_uq).reshape(B, S, MLA_HEADS, QK_DIM)
    kv = (c_kv @ w_ukv).reshape(B, S, MLA_HEADS, QK_NOPE_DIM + V_HEAD_DIM)
    k_nope, v = jnp.split(kv, [QK_NOPE_DIM], axis=-1)
    k_rope = jnp.broadcast_to(k_r[:, :, None, :], (B, S, MLA_HEADS, QK_ROPE_DIM))
    k = jnp.concatenate([k_nope, k_rope], axis=-1)
    q = rms_norm(q, g_q_head)
    k = rms_norm(k, g_k_head)
    cos, sin = rope_tables(S)
    q = jnp.concatenate([q[..., :QK_NOPE_DIM], apply_rope(q[..., QK_NOPE_DIM:], cos, sin)], axis=-1)
    k = jnp.concatenate([k[..., :QK_NOPE_DIM], apply_rope(k[..., QK_NOPE_DIM:], cos, sin)], axis=-1)
    attn = bidirectional_attention(q, k, v)
    pool = multi_scale_pool(u, w_pool, pool_scale)
    return jnp.concatenate([attn, pool], axis=-1) @ w_out


def hier_moe(a, w_group_router, b_group_router, w_expert_router, b_expert_router, w_gate, w_up, w_down):
    B, S, D = a.shape
    T = B * S
    x = a.reshape(T, D)
    gp = jax.nn.softmax((x @ w_group_router).astype(jnp.float32) + b_group_router.astype(jnp.float32), axis=-1)
    g_val, g_idx = lax.top_k(gp, 1)
    el = ((x @ w_expert_router).astype(jnp.float32) + b_expert_router.astype(jnp.float32))
    el = el.reshape(T, N_GROUPS, EXPERTS_PER_GROUP)
    el_sel = jnp.take_along_axis(el, g_idx[:, :, None], axis=1)[:, 0]
    ep = jax.nn.softmax(el_sel, axis=-1)
    e_val, e_idx = lax.top_k(ep, TOP_K)
    e_val = e_val / jnp.sum(e_val, axis=-1, keepdims=True)
    gate = g_val * e_val
    expert_id = g_idx * EXPERTS_PER_GROUP + e_idx
    combine = jnp.sum(jax.nn.one_hot(expert_id, N_EXPERTS, dtype=jnp.float32) * gate[..., None], axis=1)
    y = jnp.zeros((T, D), jnp.float32)
    for e in range(N_EXPERTS):
        hdn = jax.nn.silu(x @ w_gate[e]) * (x @ w_up[e])
        y = y + combine[:, e:e + 1] * (hdn @ w_down[e]).astype(jnp.float32)
    return y.astype(a.dtype).reshape(B, S, D)


def encoder(x, g_attn_norm, w_in, g_q_lat, g_kv_lat, w_uq, w_ukv, g_q_head, g_k_head, w_pool, pool_scale,
            w_out, g_ffn_norm, w_group_router, b_group_router, w_expert_router, b_expert_router,
            w_gate, w_up, w_down):
    for l in range(DEPTH):
        h = x + token_mixer(rms_norm(x, g_attn_norm[l]), w_in[l], g_q_lat[l], g_kv_lat[l], w_uq[l], w_ukv[l],
                            g_q_head[l], g_k_head[l], w_pool[l], pool_scale[l], w_out[l])
        x = h + hier_moe(rms_norm(h, g_ffn_norm[l]), w_group_router[l], b_group_router[l],
                         w_expert_router[l], b_expert_router[l], w_gate[l], w_up[l], w_down[l])
    return x


def setup_inputs(seed: int = 0) -> dict:
    key = jax.random.key(seed)
    ks = jax.random.split(key, 22)
    f32 = jnp.float32

    def nrm(k, shape, fan_in):
        return jax.random.normal(k, shape, f32) * (fan_in ** -0.5)

    def gain(k, shape):
        return 1.0 + 0.1 * jax.random.normal(k, shape, f32)

    return {
        "x_prompt": jax.random.normal(ks[0], (BATCH, SEQ, D_MODEL), f32),
        "x_sample": jax.random.normal(ks[1], (DEC_BATCH, DEC_SEQ, D_MODEL), f32),
        "g_attn_norm": gain(ks[2], (DEPTH, D_MODEL)),
        "w_in": nrm(ks[3], (DEPTH, D_MODEL, IN_WIDTH), D_MODEL),
        "g_q_lat": gain(ks[4], (DEPTH, Q_LORA_RANK)),
        "g_kv_lat": gain(ks[5], (DEPTH, KV_LORA_RANK)),
        "w_uq": nrm(ks[6], (DEPTH, Q_LORA_RANK, MLA_HEADS * QK_DIM), Q_LORA_RANK),
        "w_ukv": nrm(ks[7], (DEPTH, KV_LORA_RANK, MLA_HEADS * (QK_NOPE_DIM + V_HEAD_DIM)), KV_LORA_RANK),
        "g_q_head": gain(ks[8], (DEPTH, QK_DIM)),
        "g_k_head": gain(ks[9], (DEPTH, QK_DIM)),
        "w_pool": nrm(ks[10], (DEPTH, N_POOL_GROUPS, POOL_CH, POOL_CH), POOL_CH),
        "pool_scale": gain(ks[11], (DEPTH, POOL_WIDTH)),
        "w_out": nrm(ks[12], (DEPTH, D_MIX, D_MODEL), D_MIX),
        "g_ffn_norm": gain(ks[13], (DEPTH, D_MODEL)),
        "w_group_router": nrm(ks[14], (DEPTH, D_MODEL, N_GROUPS), D_MODEL),
        "b_group_router": 0.01 * jax.random.normal(ks[15], (DEPTH, N_GROUPS), f32),
        "w_expert_router": nrm(ks[16], (DEPTH, D_MODEL, N_EXPERTS), D_MODEL),
        "b_expert_router": 0.01 * jax.random.normal(ks[17], (DEPTH, N_EXPERTS), f32),
        "w_gate": nrm(ks[18], (DEPTH, N_EXPERTS, D_MODEL, EXPERT_HIDDEN), D_MODEL),
        "w_up": nrm(ks[19], (DEPTH, N_EXPERTS, D_MODEL, EXPERT_HIDDEN), D_MODEL),
        "w_down": nrm(ks[20], (DEPTH, N_EXPERTS, EXPERT_HIDDEN, D_MODEL), EXPERT_HIDDEN),
    }


def reference(x_prompt, x_sample, g_attn_norm, w_in, g_q_lat, g_kv_lat, w_uq, w_ukv, g_q_head, g_k_head,
              w_pool, pool_scale, w_out, g_ffn_norm, w_group_router, b_group_router, w_expert_router,
              b_expert_router, w_gate, w_up, w_down):
    y_prompt = encoder(x_prompt, g_attn_norm, w_in, g_q_lat, g_kv_lat, w_uq, w_ukv, g_q_head, g_k_head,
                       w_pool, pool_scale, w_out, g_ffn_norm, w_group_router, b_group_router,
                       w_expert_router, b_expert_router, w_gate, w_up, w_down)
    y_sample = encoder(x_sample, g_attn_norm, w_in, g_q_lat, g_kv_lat, w_uq, w_ukv, g_q_head, g_k_head,
                       w_pool, pool_scale, w_out, g_ffn_norm, w_group_router, b_group_router,
                       w_expert_router, b_expert_router, w_gate, w_up, w_down)
    return (y_prompt, y_sample)
```

```python
import functools
import math

import jax
import jax.numpy as jnp
from jax import lax
from jax.experimental import pallas as pl
from jax.experimental.pallas import tpu as pltpu

D_MODEL = 1024
V_HEAD_DIM = 128
MLA_HEADS = 4
QK_NOPE_DIM = 128
QK_ROPE_DIM = 64
QK_DIM = QK_NOPE_DIM + QK_ROPE_DIM
Q_LORA_RANK = 384
KV_LORA_RANK = 256
ROPE_THETA = 10000.0
POOL_WINDOWS = (2, 4, 8, 16)
POOL_CH = 128
POOL_WIDTH = POOL_CH * len(POOL_WINDOWS)
ATTN_WIDTH = MLA_HEADS * V_HEAD_DIM
N_GROUPS = 4
EXPERTS_PER_GROUP = 8
N_EXPERTS = N_GROUPS * EXPERTS_PER_GROUP
EXPERT_HIDDEN = 256
EPS = 1e-6

LANES = 128
SUBLANES = 8
ROW_TILES = D_MODEL // LANES
HEAD_PAD = 256
POOL_HALO = 8
ROUTER_ROWS = 8 + N_EXPERTS

VMEM_LIMIT = 52 * 1024 * 1024

TS_PRE = 512
TQ = 512
TK = 512
TS_POST = 512
TT_DISPATCH = 512
TM_FFN = 512
TC_COMBINE = 512

_NT = (((1,), (1,)), ((), ()))


def _bf16(x):
    return x.astype(jnp.bfloat16)


def _pre_kernel(x_ref, tab_ref, gattn_ref, win_ref, gq_lat_ref, gkv_lat_ref, wuq_ref, wukv_ref,
                gq_ref, gkn_ref, gkr_ref, q_ref, k_ref, v_ref, u_ref):
    x = x_ref[0]
    r = lax.rsqrt(jnp.mean(x * x, axis=-1, keepdims=True) + EPS)
    a = _bf16(x * r * gattn_ref[...])
    p = jnp.dot(a, win_ref[...], preferred_element_type=jnp.float32)
    o_kv = Q_LORA_RANK
    o_kr = o_kv + KV_LORA_RANK
    o_u = o_kr + LANES
    c_q = p[:, :o_kv]
    c_kv = p[:, o_kv:o_kr]
    kr = p[:, o_kr:o_u]
    u_ref[0] = p[:, o_u:]

    rq_lat = lax.rsqrt(jnp.mean(c_q * c_q, axis=-1, keepdims=True) + EPS)
    cq = _bf16(c_q * rq_lat * gq_lat_ref[...])
    rkv_lat = lax.rsqrt(jnp.mean(c_kv * c_kv, axis=-1, keepdims=True) + EPS)
    ckv = _bf16(c_kv * rkv_lat * gkv_lat_ref[...])
    qf = jnp.dot(cq, wuq_ref[...], preferred_element_type=jnp.float32)
    kvf = jnp.dot(ckv, wukv_ref[...], preferred_element_type=jnp.float32)

    tab = tab_ref[...]
    lane = lax.broadcasted_iota(jnp.int32, tab.shape, 1)
    first_half = lane < QK_ROPE_DIM
    xk = kr * gkr_ref[...] * tab
    yk = xk + pltpu.roll(xk, QK_ROPE_DIM, axis=1)
    ssq_kr = jnp.sum(jnp.where(first_half, kr * kr, 0.0), axis=-1, keepdims=True)
    gq = gq_ref[...]
    inv_dim = 1.0 / QK_DIM
    for h in range(MLA_HEADS):
        qn = qf[:, h * HEAD_PAD:h * HEAD_PAD + LANES]
        qr = qf[:, h * HEAD_PAD + LANES:(h + 1) * HEAD_PAD]
        ssq = (jnp.sum(qn * qn, axis=-1, keepdims=True)
               + jnp.sum(jnp.where(first_half, qr * qr, 0.0), axis=-1, keepdims=True))
        rq = lax.rsqrt(ssq * inv_dim + EPS)
        q_ref[0, h, :, :LANES] = _bf16(qn * rq * gq[:, :LANES])
        q_ref[0, h, :, LANES:] = _bf16(qr * rq * gq[:, LANES:] * tab)
        kn = kvf[:, h * HEAD_PAD:h * HEAD_PAD + LANES]
        ssqk = jnp.sum(kn * kn, axis=-1, keepdims=True) + ssq_kr
        rk = lax.rsqrt(ssqk * inv_dim + EPS)
        k_ref[0, h, :, :LANES] = _bf16(kn * rk * gkn_ref[...])
        k_ref[0, h, :, LANES:] = _bf16(yk * rk)
        v_ref[0, h] = _bf16(kvf[:, h * HEAD_PAD + LANES:(h + 1) * HEAD_PAD])


def _pre(x, tab, gattn, win, gq_lat, gkv_lat, wuq, wukv, gq, gkn, gkr):
    B, S, D = x.shape
    ts = TS_PRE
    full = lambda arr: pl.BlockSpec(arr.shape, lambda b, i: (0,) * arr.ndim)
    return pl.pallas_call(
        _pre_kernel,
        grid=(B, S // ts),
        in_specs=[
            pl.BlockSpec((1, ts, D), lambda b, i: (b, i, 0)),
            pl.BlockSpec((ts, LANES), lambda b, i: (i, 0)),
            full(gattn), full(win), full(gq_lat), full(gkv_lat), full(wuq), full(wukv),
            full(gq), full(gkn), full(gkr),
        ],
        out_specs=[
            pl.BlockSpec((1, MLA_HEADS, ts, HEAD_PAD), lambda b, i: (b, 0, i, 0)),
            pl.BlockSpec((1, MLA_HEADS, ts, HEAD_PAD), lambda b, i: (b, 0, i, 0)),
            pl.BlockSpec((1, MLA_HEADS, ts, V_HEAD_DIM), lambda b, i: (b, 0, i, 0)),
            pl.BlockSpec((1, ts, POOL_WIDTH), lambda b, i: (b, i, 0)),
        ],
        out_shape=[
            jax.ShapeDtypeStruct((B, MLA_HEADS, S, HEAD_PAD), jnp.bfloat16),
            jax.ShapeDtypeStruct((B, MLA_HEADS, S, HEAD_PAD), jnp.bfloat16),
            jax.ShapeDtypeStruct((B, MLA_HEADS, S, V_HEAD_DIM), jnp.bfloat16),
            jax.ShapeDtypeStruct((B, S, POOL_WIDTH), jnp.float32),
        ],
        compiler_params=pltpu.CompilerParams(
            dimension_semantics=("parallel", "parallel"), vmem_limit_bytes=VMEM_LIMIT),
        name="pre",
    )(x, tab, gattn, win, gq_lat, gkv_lat, wuq, wukv, gq, gkn, gkr)


def _attn_kernel(q_ref, k_ref, v_ref, o_ref, m_ref, l_ref, acc_ref, *, n_kv, tk):
    q = q_ref[0, 0]
    m_ref[...] = jnp.full(m_ref.shape, -jnp.inf, jnp.float32)
    l_ref[...] = jnp.zeros(l_ref.shape, jnp.float32)
    acc_ref[...] = jnp.zeros(acc_ref.shape, jnp.float32)

    def body(j, carry):
        start = pl.multiple_of(j * tk, tk)
        k = k_ref[0, 0, pl.ds(start, tk), :]
        v = v_ref[0, 0, pl.ds(start, tk), :]
        s = lax.dot_general(q, k, _NT, preferred_element_type=jnp.float32)
        m_prev = m_ref[...]
        m_new = jnp.maximum(m_prev, jnp.max(s, axis=-1, keepdims=True))
        alpha = jnp.exp2(m_prev - m_new)
        p = jnp.exp2(s - m_new)
        l_ref[...] = alpha * l_ref[...] + jnp.sum(p, axis=-1, keepdims=True)
        acc_ref[...] = alpha * acc_ref[...] + jnp.dot(_bf16(p), v, preferred_element_type=jnp.float32)
        m_ref[...] = m_new
        return carry

    lax.fori_loop(0, n_kv, body, 0)
    o_ref[0] = _bf16(acc_ref[...] / l_ref[...])


def _attention(q, k, v):
    B, H, S, _ = q.shape
    tq, tk = TQ, TK
    kern = functools.partial(_attn_kernel, n_kv=S // tk, tk=tk)
    return pl.pallas_call(
        kern,
        grid=(B, H, S // tq),
        in_specs=[
            pl.BlockSpec((1, 1, tq, HEAD_PAD), lambda b, h, i: (b, h, i, 0)),
            pl.BlockSpec((1, 1, S, HEAD_PAD), lambda b, h, i: (b, h, 0, 0)),
            pl.BlockSpec((1, 1, S, V_HEAD_DIM), lambda b, h, i: (b, h, 0, 0)),
        ],
        out_specs=pl.BlockSpec((1, tq, V_HEAD_DIM), lambda b, h, i: (b, i, h)),
        out_shape=jax.ShapeDtypeStruct((B, S, ATTN_WIDTH), jnp.bfloat16),
        scratch_shapes=[
            pltpu.VMEM((tq, 1), jnp.float32),
            pltpu.VMEM((tq, 1), jnp.float32),
            pltpu.VMEM((tq, V_HEAD_DIM), jnp.float32),
        ],
        compiler_params=pltpu.CompilerParams(
            dimension_semantics=("parallel", "parallel", "parallel"), vmem_limit_bytes=VMEM_LIMIT),
        name="attn",
    )(q, k, v)


def _post_kernel(x_ref, attn_ref, u_ref, up_ref, un_ref, wpool_ref, ps_ref, woa_ref, wop_ref,
                 gffn_ref, wr_ref, br_ref, tri_ref,
                 h_ref, a2_ref, route_ref, cnt_ref, cnt_scr, *, seq_len):
    b = pl.program_id(0)
    i = pl.program_id(1)
    n_i = pl.num_programs(1)
    ts = u_ref.shape[1]
    n_ext = ts + 2 * POOL_HALO

    @pl.when(jnp.logical_and(b == 0, i == 0))
    def _():
        cnt_scr[...] = jnp.zeros(cnt_scr.shape, jnp.int32)

    u = u_ref[0]
    up = jnp.where(i > 0, up_ref[0], 0.0)
    un = jnp.where(i < n_i - 1, un_ref[0], 0.0)
    uext = jnp.concatenate([up, u, un], axis=0)
    pos = i * ts + lax.broadcasted_iota(jnp.int32, (ts, 1), 0)
    mix = jnp.dot(attn_ref[0], woa_ref[...], preferred_element_type=jnp.float32)
    for g, w in enumerate(POOL_WINDOWS):
        lo = w // 2
        hi = w - lo - 1
        ug = uext[:, g * POOL_CH:(g + 1) * POOL_CH]
        s = ug + pltpu.roll(ug, 1, axis=0)
        half = 1
        while 2 * half < w:
            s = pltpu.roll(s, half, axis=0) + pltpu.roll(s, n_ext - half, axis=0)
            half *= 2
        win = s[POOL_HALO:POOL_HALO + ts]
        cnt = (jnp.minimum(pos + hi + 1, seq_len) - jnp.maximum(pos - lo, 0)).astype(jnp.float32)
        d = win / cnt - u[:, g * POOL_CH:(g + 1) * POOL_CH]
        mixed = jnp.dot(_bf16(d), wpool_ref[g], preferred_element_type=jnp.float32)
        pool_g = mixed * ps_ref[:, g * POOL_CH:(g + 1) * POOL_CH]
        mix = mix + jnp.dot(_bf16(pool_g), wop_ref[g * POOL_CH:(g + 1) * POOL_CH, :],
                            preferred_element_type=jnp.float32)

    h = x_ref[0] + mix
    h_ref[0] = h
    r = lax.rsqrt(jnp.mean(h * h, axis=-1, keepdims=True) + EPS)
    a2 = h * r * gffn_ref[...]
    for s8 in range(ROW_TILES):
        a2_ref[pl.ds(s8, ts, stride=SUBLANES), :] = a2[:, s8 * LANES:(s8 + 1) * LANES]

    logits = lax.dot_general(wr_ref[...], a2, _NT, preferred_element_type=jnp.float32,
                             precision=lax.Precision.HIGHEST) + br_ref[...]
    gl = logits[:N_GROUPS]
    ge = jnp.exp(gl - jnp.max(gl, axis=0, keepdims=True))
    gp = ge / jnp.sum(ge, axis=0, keepdims=True)
    g_val = jnp.max(gp, axis=0, keepdims=True)
    gi = lax.broadcasted_iota(jnp.int32, gp.shape, 0)
    g_idx = jnp.min(jnp.where(gp == g_val, gi, N_GROUPS), axis=0, keepdims=True)

    el = logits[SUBLANES:]
    ei = lax.broadcasted_iota(jnp.int32, el.shape, 0)
    in_grp = (ei // EXPERTS_PER_GROUP) == g_idx
    el_sel = jnp.where(in_grp, el, -jnp.inf)
    ex = jnp.exp(el_sel - jnp.max(el_sel, axis=0, keepdims=True))
    ep = ex / jnp.sum(ex, axis=0, keepdims=True)
    ep = jnp.where(in_grp, ep, -1.0)
    v1 = jnp.max(ep, axis=0, keepdims=True)
    i1 = jnp.min(jnp.where(ep == v1, ei, N_EXPERTS), axis=0, keepdims=True)
    ep2 = jnp.where(ei == i1, -1.0, ep)
    v2 = jnp.max(ep2, axis=0, keepdims=True)
    i2 = jnp.min(jnp.where(ep2 == v2, ei, N_EXPERTS), axis=0, keepdims=True)
    den = v1 + v2
    w1 = g_val * (v1 / den)
    w2 = g_val * (v2 / den)

    sel1 = ei == i1
    sel2 = ei == i2
    onehot = jnp.where(sel1, 1.0, jnp.where(sel2, 1.0, 0.0))
    before = jnp.dot(_bf16(onehot), tri_ref[...], preferred_element_type=jnp.float32)
    rank = cnt_scr[:, 0:1] + before.astype(jnp.int32)
    r1 = jnp.sum(jnp.where(sel1, rank, 0), axis=0, keepdims=True)
    r2 = jnp.sum(jnp.where(sel2, rank, 0), axis=0, keepdims=True)
    tot = jnp.sum(onehot, axis=1, keepdims=True).astype(jnp.int32)
    cnt_new = cnt_scr[...] + tot
    cnt_scr[...] = cnt_new
    cnt_ref[...] = cnt_new

    route_ref[0:1, :] = i1
    route_ref[1:2, :] = i2
    route_ref[2:3, :] = r1
    route_ref[3:4, :] = r2
    route_ref[4:5, :] = pltpu.bitcast(w1, jnp.int32)
    route_ref[5:6, :] = pltpu.bitcast(w2, jnp.int32)
    route_ref[6:8, :] = jnp.zeros((2, ts), jnp.int32)


def _post(x, attn, u, wpool, ps, woa, wop, gffn, wr, br, tri):
    B, S, D = x.shape
    ts = TS_POST
    n_i = S // ts
    T = B * S
    hb = ts // POOL_HALO
    full = lambda arr: pl.BlockSpec(arr.shape, lambda b, i: (0,) * arr.ndim)
    kern = functools.partial(_post_kernel, seq_len=S)
    return pl.pallas_call(
        kern,
        grid=(B, n_i),
        in_specs=[
            pl.BlockSpec((1, ts, D), lambda b, i: (b, i, 0)),
            pl.BlockSpec((1, ts, ATTN_WIDTH), lambda b, i: (b, i, 0)),
            pl.BlockSpec((1, ts, POOL_WIDTH), lambda b, i: (b, i, 0)),
            pl.BlockSpec((1, POOL_HALO, POOL_WIDTH), lambda b, i: (b, jnp.maximum(i * hb - 1, 0), 0)),
            pl.BlockSpec((1, POOL_HALO, POOL_WIDTH),
                         lambda b, i: (b, jnp.minimum((i + 1) * hb, S // POOL_HALO - 1), 0)),
            full(wpool), full(ps), full(woa), full(wop), full(gffn), full(wr), full(br), full(tri),
        ],
        out_specs=[
            pl.BlockSpec((1, ts, D), lambda b, i: (b, i, 0)),
            pl.BlockSpec((ts * ROW_TILES, LANES), lambda b, i: (b * n_i + i, 0)),
            pl.BlockSpec((SUBLANES, ts), lambda b, i: (0, b * n_i + i)),
            pl.BlockSpec((N_EXPERTS, LANES), lambda b, i: (0, 0)),
        ],
        out_shape=[
            jax.ShapeDtypeStruct((B, S, D), jnp.float32),
            jax.ShapeDtypeStruct((T * ROW_TILES, LANES), jnp.float32),
            jax.ShapeDtypeStruct((SUBLANES, T), jnp.int32),
            jax.ShapeDtypeStruct((N_EXPERTS, LANES), jnp.int32),
        ],
        scratch_shapes=[pltpu.VMEM((N_EXPERTS, LANES), jnp.int32)],
        compiler_params=pltpu.CompilerParams(
            dimension_semantics=("arbitrary", "arbitrary"), vmem_limit_bytes=VMEM_LIMIT),
        name="post",
    )(x, attn, u, u, u, wpool, ps, woa, wop, gffn, wr, br, tri)


def _row_copy(src, src_row, dst, dst_row, sem):
    return pltpu.make_async_copy(
        src.at[pl.ds(pl.multiple_of(src_row * ROW_TILES, ROW_TILES), ROW_TILES)],
        dst.at[pl.ds(pl.multiple_of(dst_row * ROW_TILES, ROW_TILES), ROW_TILES)], sem)


def _dispatch_kernel(dest_ref, a2_ref, xs_ref, sem, *, n_tok, tt):
    base = pl.program_id(0) * tt

    def body(t, carry):
        _row_copy(a2_ref, t, xs_ref, dest_ref[base + t], sem).start()
        _row_copy(a2_ref, t, xs_ref, dest_ref[n_tok + base + t], sem).start()
        return carry

    lax.fori_loop(0, tt, body, 0)
    whole = pltpu.make_async_copy(a2_ref, xs_ref.at[pl.ds(0, tt * ROW_TILES)], sem)
    whole.wait()
    whole.wait()


def _dispatch(dest, a2_tiles):
    n_tok = a2_tiles.shape[0] // ROW_TILES
    tt = TT_DISPATCH
    kern = functools.partial(_dispatch_kernel, n_tok=n_tok, tt=tt)
    return pl.pallas_call(
        kern,
        grid_spec=pltpu.PrefetchScalarGridSpec(
            num_scalar_prefetch=1,
            grid=(n_tok // tt,),
            in_specs=[pl.BlockSpec((tt * ROW_TILES, LANES), lambda i, dest: (i, 0))],
            out_specs=pl.BlockSpec(memory_space=pl.ANY),
            scratch_shapes=[pltpu.SemaphoreType.DMA],
        ),
        out_shape=jax.ShapeDtypeStruct((2 * n_tok * ROW_TILES, LANES), jnp.float32),
        compiler_params=pltpu.CompilerParams(
            dimension_semantics=("arbitrary",), has_side_effects=True),
        name="dispatch",
    )(dest, a2_tiles)


def _ffn_kernel(wt_ref, we_ref, wlo_ref, whi_ref, wfirst_ref, xs_ref, wg_ref, wu_ref, wd_ref, o_ref):
    w = pl.program_id(0)
    tm = xs_ref.shape[0] // ROW_TILES
    x = _bf16(jnp.concatenate(
        [xs_ref[pl.ds(s8, tm, stride=SUBLANES), :] for s8 in range(ROW_TILES)], axis=1))
    g = jnp.dot(x, wg_ref[0], preferred_element_type=jnp.float32)
    up = jnp.dot(x, wu_ref[0], preferred_element_type=jnp.float32)
    hdn = g / (1.0 + jnp.exp(-g)) * up
    out = jnp.dot(_bf16(hdn), wd_ref[0], preferred_element_type=jnp.float32)
    row = lax.broadcasted_iota(jnp.int32, (tm, 1), 0)
    mine = jnp.logical_and(row >= wlo_ref[w], row < whi_ref[w])

    @pl.when(wfirst_ref[w] == 1)
    def _():
        for s8 in range(ROW_TILES):
            o_ref[pl.ds(s8, tm, stride=SUBLANES), :] = jnp.where(mine, out[:, s8 * LANES:(s8 + 1) * LANES], 0.0)

    @pl.when(wfirst_ref[w] == 0)
    def _():
        for s8 in range(ROW_TILES):
            rows = pl.ds(s8, tm, stride=SUBLANES)
            o_ref[rows, :] = jnp.where(mine, out[:, s8 * LANES:(s8 + 1) * LANES], o_ref[rows, :])


def _ffn(plan, xs, wg, wu, wd):
    tm = TM_FFN
    n_items = plan[0].shape[0]
    tile = lambda w, wt, we, wlo, whi, wf: (wt[w], 0)
    expert = lambda w, wt, we, wlo, whi, wf: (we[w], 0, 0)
    return pl.pallas_call(
        _ffn_kernel,
        grid_spec=pltpu.PrefetchScalarGridSpec(
            num_scalar_prefetch=5,
            grid=(n_items,),
            in_specs=[
                pl.BlockSpec((tm * ROW_TILES, LANES), tile),
                pl.BlockSpec((1, D_MODEL, EXPERT_HIDDEN), expert),
                pl.BlockSpec((1, D_MODEL, EXPERT_HIDDEN), expert),
                pl.BlockSpec((1, EXPERT_HIDDEN, D_MODEL), expert),
            ],
            out_specs=pl.BlockSpec((tm * ROW_TILES, LANES), tile),
        ),
        out_shape=jax.ShapeDtypeStruct(xs.shape, jnp.float32),
        compiler_params=pltpu.CompilerParams(
            dimension_semantics=("arbitrary",), vmem_limit_bytes=VMEM_LIMIT),
        name="ffn",
    )(*plan, xs, wg, wu, wd)


def _combine_kernel(dest_ref, h_ref, gates_ref, os_ref, y_ref, buf, sem, *, n_tok, tc):
    base = pl.program_id(0) * tc

    def body(t, carry):
        _row_copy(os_ref, dest_ref[base + t], buf, t, sem).start()
        _row_copy(os_ref, dest_ref[n_tok + base + t], buf, tc + t, sem).start()
        return carry

    lax.fori_loop(0, tc, body, 0)
    pltpu.make_async_copy(os_ref.at[pl.ds(0, 2 * tc * ROW_TILES)], buf, sem).wait()
    w1 = gates_ref[:, 0:1]
    w2 = gates_ref[:, 1:2]
    for s8 in range(ROW_TILES):
        r1 = buf[pl.ds(s8, tc, stride=SUBLANES), :]
        r2 = buf[pl.ds(tc * ROW_TILES + s8, tc, stride=SUBLANES), :]
        cols = slice(s8 * LANES, (s8 + 1) * LANES)
        y_ref[:, cols] = h_ref[:, cols] + (w1 * r1 + w2 * r2)


def _combine(dest, h, gates, out_sorted):
    n_tok = h.shape[0]
    tc = TC_COMBINE
    kern = functools.partial(_combine_kernel, n_tok=n_tok, tc=tc)
    return pl.pallas_call(
        kern,
        grid_spec=pltpu.PrefetchScalarGridSpec(
            num_scalar_prefetch=1,
            grid=(n_tok // tc,),
            in_specs=[
                pl.BlockSpec((tc, D_MODEL), lambda i, dest: (i, 0)),
                pl.BlockSpec((tc, 2), lambda i, dest: (i, 0)),
                pl.BlockSpec(memory_space=pl.ANY),
            ],
            out_specs=pl.BlockSpec((tc, D_MODEL), lambda i, dest: (i, 0)),
            scratch_shapes=[
                pltpu.VMEM((2 * tc * ROW_TILES, LANES), jnp.float32),
                pltpu.SemaphoreType.DMA,
            ],
        ),
        out_shape=jax.ShapeDtypeStruct(h.shape, jnp.float32),
        compiler_params=pltpu.CompilerParams(
            dimension_semantics=("arbitrary",), vmem_limit_bytes=VMEM_LIMIT),
        name="combine",
    )(dest, h, gates, out_sorted)


def _rope_table(seq):
    inv_freq = 1.0 / (ROPE_THETA ** (jnp.arange(0, QK_ROPE_DIM, 2, dtype=jnp.float32) / QK_ROPE_DIM))
    ang = jnp.arange(seq, dtype=jnp.float32)[:, None] * inv_freq[None, :]
    emb = jnp.concatenate([ang, ang], axis=-1)
    return jnp.concatenate([jnp.cos(emb), jnp.sin(emb)], axis=-1)


def _prep_layer(w_in, g_q_lat, g_kv_lat, w_uq, w_ukv, g_q_head, g_k_head, w_pool, pool_scale, w_out,
                g_attn_norm, g_ffn_norm, w_group_router, b_group_router, w_expert_router, b_expert_router,
                w_gate, w_up, w_down):
    half = QK_ROPE_DIM // 2
    perm = jnp.concatenate([jnp.arange(half, QK_ROPE_DIM), jnp.arange(0, half)])
    sign = jnp.concatenate([-jnp.ones((half,), jnp.float32), jnp.ones((half,), jnp.float32)])
    o_kv = Q_LORA_RANK
    o_kr = o_kv + KV_LORA_RANK
    o_u = o_kr + QK_ROPE_DIM
    wkr = w_in[:, o_kr:o_u]
    win = _bf16(jnp.concatenate([w_in[:, :o_kr], wkr, wkr[:, perm], w_in[:, o_u:]], axis=1))
    wuq3 = w_uq.reshape(Q_LORA_RANK, MLA_HEADS, QK_DIM)
    rope_cols = wuq3[:, :, QK_NOPE_DIM:]
    wuq = _bf16(jnp.concatenate([wuq3, rope_cols[:, :, perm]], axis=2).reshape(Q_LORA_RANK, MLA_HEADS * HEAD_PAD))
    q_scale = (QK_DIM ** -0.5) * math.log2(math.e)
    gq_rope = g_q_head[QK_NOPE_DIM:]
    gq = (jnp.concatenate([g_q_head, gq_rope[perm] * sign]) * q_scale)[None, :]
    gk_rope = g_k_head[QK_NOPE_DIM:]
    gkn = g_k_head[None, :QK_NOPE_DIM]
    gkr = jnp.concatenate([gk_rope, gk_rope[perm] * sign])[None, :]
    wr = jnp.zeros((ROUTER_ROWS, D_MODEL), jnp.float32)
    wr = wr.at[:N_GROUPS].set(w_group_router.T).at[SUBLANES:].set(w_expert_router.T)
    br = jnp.zeros((ROUTER_ROWS, 1), jnp.float32)
    br = br.at[:N_GROUPS, 0].set(b_group_router).at[SUBLANES:, 0].set(b_expert_router)
    return dict(
        gattn=g_attn_norm[None, :], win=win, gq_lat=g_q_lat[None, :], gkv_lat=g_kv_lat[None, :],
        wuq=wuq, wukv=_bf16(w_ukv), gq=gq, gkn=gkn, gkr=gkr,
        wpool=_bf16(w_pool), ps=pool_scale[None, :], woa=_bf16(w_out[:ATTN_WIDTH]), wop=_bf16(w_out[ATTN_WIDTH:]),
        gffn=g_ffn_norm[None, :], wr=wr, br=br,
        wg=_bf16(w_gate), wu=_bf16(w_up), wd=_bf16(w_down),
    )


def _route_plan(route, counts, tm):
    n_tok = route.shape[1]
    n_rows = 2 * n_tok
    n_tiles = n_rows // tm
    seg_end = jnp.cumsum(counts)
    seg_start = seg_end - counts
    dest = (jnp.take(seg_start, route[0:2], axis=0) + route[2:4]).reshape(n_rows).astype(jnp.int32)
    first_tile = seg_start // tm
    n_items = jnp.where(counts > 0, (seg_end - 1) // tm - first_tile + 1, 0)
    item_end = jnp.cumsum(n_items)
    item_start = item_end - n_items
    w = jnp.arange(n_tiles + N_EXPERTS - 1, dtype=jnp.int32)
    used = w < item_end[-1]
    last_expert = jnp.max(jnp.where(counts > 0, jnp.arange(N_EXPERTS), 0))
    e_w = jnp.where(used, jnp.minimum(jnp.sum(w[:, None] >= item_end[None, :], axis=1), N_EXPERTS - 1),
                    last_expert).astype(jnp.int32)
    t_w = jnp.where(used, jnp.take(first_tile, e_w) + w - jnp.take(item_start, e_w), n_tiles - 1).astype(jnp.int32)
    lo_w = jnp.where(used, jnp.clip(jnp.take(seg_start, e_w) - t_w * tm, 0, tm), 0).astype(jnp.int32)
    hi_w = jnp.where(used, jnp.clip(jnp.take(seg_end, e_w) - t_w * tm, 0, tm), 0).astype(jnp.int32)
    first_w = jnp.concatenate([jnp.ones((1,), jnp.int32), (t_w[1:] != t_w[:-1]).astype(jnp.int32)])
    gates = lax.bitcast_convert_type(route[4:6], jnp.float32).T
    return dest, (t_w, e_w, lo_w, hi_w, first_w), gates


def _encoder_layer(x, p):
    B, S, D = x.shape
    tab = _rope_table(S)
    q, k, v, u = _pre(x, tab, p["gattn"], p["win"], p["gq_lat"], p["gkv_lat"], p["wuq"], p["wukv"],
                      p["gq"], p["gkn"], p["gkr"])
    attn = _attention(q, k, v)
    ts = TS_POST
    tri = _bf16(jnp.arange(ts)[:, None] < jnp.arange(ts)[None, :])
    h, a2_tiles, route, cnt = _post(x, attn, u, p["wpool"], p["ps"], p["woa"], p["wop"], p["gffn"],
                                    p["wr"], p["br"], tri)
    dest, plan, gates = _route_plan(route, cnt[:, 0], TM_FFN)
    xs = _dispatch(dest, a2_tiles)
    out_sorted = _ffn(plan, xs, p["wg"], p["wu"], p["wd"])
    y = _combine(dest, h.reshape(B * S, D), gates, out_sorted)
    return y.reshape(B, S, D)


def kernel(x_prompt, x_sample, g_attn_norm, w_in, g_q_lat, g_kv_lat, w_uq, w_ukv, g_q_head, g_k_head, w_pool, pool_scale, w_out, g_ffn_norm, w_group_router, b_group_router, w_expert_router, b_expert_router, w_gate, w_up, w_down):
    depth = w_in.shape[0]
    layers = [
        _prep_layer(w_in[l], g_q_lat[l], g_kv_lat[l], w_uq[l], w_ukv[l], g_q_head[l], g_k_head[l], w_pool[l],
                    pool_scale[l], w_out[l], g_attn_norm[l], g_ffn_norm[l], w_group_router[l],
                    b_group_router[l], w_expert_router[l], b_expert_router[l], w_gate[l], w_up[l], w_down[l])
        for l in range(depth)
    ]
    outs = []
    for x in (x_prompt, x_sample):
        for p in layers:
            x = _encoder_layer(x, p)
        outs.append(x)
    return tuple(outs)
```

```python
import functools
import math

import jax
import jax.numpy as jnp
from jax import lax
from jax.experimental import pallas as pl
from jax.experimental.pallas import tpu as pltpu

D_MODEL = 1024
V_HEAD_DIM = 128
MLA_HEADS = 4
QK_NOPE_DIM = 128
QK_ROPE_DIM = 64
QK_DIM = QK_NOPE_DIM + QK_ROPE_DIM
Q_LORA_RANK = 384
KV_LORA_RANK = 256
ROPE_THETA = 10000.0
POOL_WINDOWS = (2, 4, 8, 16)
POOL_CH = 128
POOL_WIDTH = POOL_CH * len(POOL_WINDOWS)
ATTN_WIDTH = MLA_HEADS * V_HEAD_DIM
N_GROUPS = 4
EXPERTS_PER_GROUP = 8
N_EXPERTS = N_GROUPS * EXPERTS_PER_GROUP
EXPERT_HIDDEN = 256
EPS = 1e-6

LANES = 128
SUBLANES = 8
HEAD_PAD = 256
POOL_HALO = 8
EXPERT_ROW0 = 16
ROUTER_ROWS = EXPERT_ROW0 + N_EXPERTS

VMEM_LIMIT = 52 * 1024 * 1024

TS_PRE = 512
TQ = 512
TK = 4096
HEADS_PER_STEP = 2
TS_POST = 512
TT_DISPATCH = 512
TM_FFN = 512
TC_COMBINE = 512
ISSUE_UNROLL = 4

_NT = (((1,), (1,)), ((), ()))


def _bf16(x):
    return x.astype(jnp.bfloat16)


def _pre_kernel(x_ref, tab_ref, gattn_ref, win_ref, gq_lat_ref, gkv_lat_ref, wuq_ref, wukv_ref,
                gq_ref, gkn_ref, gkr_ref, q_ref, k_ref, vt_ref, u_ref):
    x = x_ref[0]
    r = lax.rsqrt(jnp.mean(x * x, axis=-1, keepdims=True) + EPS)
    a = _bf16(x * r * gattn_ref[...])
    p = jnp.dot(a, win_ref[...], preferred_element_type=jnp.float32)
    o_kv = Q_LORA_RANK
    o_kr = o_kv + KV_LORA_RANK
    o_u = o_kr + LANES
    c_q = p[:, :o_kv]
    c_kv = p[:, o_kv:o_kr]
    kr = p[:, o_kr:o_u]
    u_ref[0] = p[:, o_u:]

    rq_lat = lax.rsqrt(jnp.mean(c_q * c_q, axis=-1, keepdims=True) + EPS)
    cq = _bf16(c_q * rq_lat * gq_lat_ref[...])
    rkv_lat = lax.rsqrt(jnp.mean(c_kv * c_kv, axis=-1, keepdims=True) + EPS)
    ckv = _bf16(c_kv * rkv_lat * gkv_lat_ref[...])
    qf = jnp.dot(cq, wuq_ref[...], preferred_element_type=jnp.float32)
    kvf = jnp.dot(ckv, wukv_ref[...], preferred_element_type=jnp.float32)

    tab = tab_ref[...]
    lane = lax.broadcasted_iota(jnp.int32, tab.shape, 1)
    first_half = lane < QK_ROPE_DIM
    xk = kr * gkr_ref[...] * tab
    yk = xk + pltpu.roll(xk, QK_ROPE_DIM, axis=1)
    ssq_kr = jnp.sum(jnp.where(first_half, kr * kr, 0.0), axis=-1, keepdims=True)
    gq = gq_ref[...]
    inv_dim = 1.0 / QK_DIM
    for h in range(MLA_HEADS):
        qn = qf[:, h * HEAD_PAD:h * HEAD_PAD + LANES]
        qr = qf[:, h * HEAD_PAD + LANES:(h + 1) * HEAD_PAD]
        ssq = (jnp.sum(qn * qn, axis=-1, keepdims=True)
               + jnp.sum(jnp.where(first_half, qr * qr, 0.0), axis=-1, keepdims=True))
        rq = lax.rsqrt(ssq * inv_dim + EPS)
        q_ref[0, h, :, :LANES] = _bf16(qn * rq * gq[:, :LANES])
        q_ref[0, h, :, LANES:] = _bf16(qr * rq * gq[:, LANES:] * tab)
        kn = kvf[:, h * HEAD_PAD:h * HEAD_PAD + LANES]
        ssqk = jnp.sum(kn * kn, axis=-1, keepdims=True) + ssq_kr
        rk = lax.rsqrt(ssqk * inv_dim + EPS)
        k_ref[0, h, :, :LANES] = _bf16(kn * rk * gkn_ref[...])
        k_ref[0, h, :, LANES:] = _bf16(yk * rk)
        vt_ref[0, h] = _bf16(kvf[:, h * HEAD_PAD + LANES:(h + 1) * HEAD_PAD].T)


def _pre(x, tab, gattn, win, gq_lat, gkv_lat, wuq, wukv, gq, gkn, gkr):
    B, S, D = x.shape
    ts = TS_PRE
    full = lambda arr: pl.BlockSpec(arr.shape, lambda b, i: (0,) * arr.ndim)
    return pl.pallas_call(
        _pre_kernel,
        grid=(B, S // ts),
        in_specs=[
            pl.BlockSpec((1, ts, D), lambda b, i: (b, i, 0)),
            pl.BlockSpec((ts, LANES), lambda b, i: (i, 0)),
            full(gattn), full(win), full(gq_lat), full(gkv_lat), full(wuq), full(wukv),
            full(gq), full(gkn), full(gkr),
        ],
        out_specs=[
            pl.BlockSpec((1, MLA_HEADS, ts, HEAD_PAD), lambda b, i: (b, 0, i, 0)),
            pl.BlockSpec((1, MLA_HEADS, ts, HEAD_PAD), lambda b, i: (b, 0, i, 0)),
            pl.BlockSpec((1, MLA_HEADS, V_HEAD_DIM, ts), lambda b, i: (b, 0, 0, i)),
            pl.BlockSpec((1, ts, POOL_WIDTH), lambda b, i: (b, i, 0)),
        ],
        out_shape=[
            jax.ShapeDtypeStruct((B, MLA_HEADS, S, HEAD_PAD), jnp.bfloat16),
            jax.ShapeDtypeStruct((B, MLA_HEADS, S, HEAD_PAD), jnp.bfloat16),
            jax.ShapeDtypeStruct((B, MLA_HEADS, V_HEAD_DIM, S), jnp.bfloat16),
            jax.ShapeDtypeStruct((B, S, POOL_WIDTH), jnp.float32),
        ],
        compiler_params=pltpu.CompilerParams(
            dimension_semantics=("parallel", "parallel"), vmem_limit_bytes=VMEM_LIMIT),
        name="pre",
    )(x, tab, gattn, win, gq_lat, gkv_lat, wuq, wukv, gq, gkn, gkr)


def _attn_kernel(q_ref, k_ref, vt_ref, o_ref, m_ref, l_ref, acc_ref, *, n_kv, tk, heads):
    for g in range(heads):
        m_ref[g] = jnp.full(m_ref.shape[1:], -jnp.inf, jnp.float32)
        l_ref[g] = jnp.zeros(l_ref.shape[1:], jnp.float32)
        acc_ref[g] = jnp.zeros(acc_ref.shape[1:], jnp.float32)

    def body(j, carry):
        start = pl.multiple_of(j * tk, tk)
        for g in range(heads):
            k = k_ref[0, g, pl.ds(start, tk), :]
            st = lax.dot_general(k, q_ref[0, g], _NT, preferred_element_type=jnp.float32)
            m_prev = m_ref[g]
            m_new = jnp.maximum(m_prev, jnp.max(st, axis=0, keepdims=True))
            alpha = jnp.exp2(m_prev - m_new)
            p = jnp.exp2(st - m_new)
            l_ref[g] = alpha * l_ref[g] + jnp.sum(p, axis=0, keepdims=True)
            acc_ref[g] = alpha * acc_ref[g] + jnp.dot(
                vt_ref[0, g, :, pl.ds(start, tk)], _bf16(p), preferred_element_type=jnp.float32)
            m_ref[g] = m_new
        return carry

    lax.fori_loop(0, n_kv, body, 0)
    for g in range(heads):
        o_ref[0, :, g * V_HEAD_DIM:(g + 1) * V_HEAD_DIM] = _bf16((acc_ref[g] / l_ref[g]).T)


def _attention(q, k, vt):
    B, H, S, _ = q.shape
    tq, tk, hg = TQ, min(TK, S), HEADS_PER_STEP
    kern = functools.partial(_attn_kernel, n_kv=S // tk, tk=tk, heads=hg)
    return pl.pallas_call(
        kern,
        grid=(B, H // hg, S // tq),
        in_specs=[
            pl.BlockSpec((1, hg, tq, HEAD_PAD), lambda b, h, i: (b, h, i, 0)),
            pl.BlockSpec((1, hg, S, HEAD_PAD), lambda b, h, i: (b, h, 0, 0)),
            pl.BlockSpec((1, hg, V_HEAD_DIM, S), lambda b, h, i: (b, h, 0, 0)),
        ],
        out_specs=pl.BlockSpec((1, tq, hg * V_HEAD_DIM), lambda b, h, i: (b, i, h)),
        out_shape=jax.ShapeDtypeStruct((B, S, ATTN_WIDTH), jnp.bfloat16),
        scratch_shapes=[
            pltpu.VMEM((hg, 1, tq), jnp.float32),
            pltpu.VMEM((hg, 1, tq), jnp.float32),
            pltpu.VMEM((hg, V_HEAD_DIM, tq), jnp.float32),
        ],
        compiler_params=pltpu.CompilerParams(
            dimension_semantics=("parallel", "parallel", "parallel"), vmem_limit_bytes=VMEM_LIMIT),
        name="attn",
    )(q, k, vt)


def _post_kernel(x_ref, attn_ref, u_ref, up_ref, un_ref, wpool_ref, ps_ref, woa_ref, wop_ref,
                 gffn_ref, wr_ref, br_ref, tri_ref,
                 h_ref, a2_ref, route_ref, cnt_ref, cnt_scr, *, seq_len):
    b = pl.program_id(0)
    i = pl.program_id(1)
    n_i = pl.num_programs(1)
    ts = u_ref.shape[1]
    n_ext = ts + 2 * POOL_HALO

    @pl.when(jnp.logical_and(b == 0, i == 0))
    def _():
        cnt_scr[...] = jnp.zeros(cnt_scr.shape, jnp.int32)

    u = u_ref[0]
    up = jnp.where(i > 0, up_ref[0], 0.0)
    un = jnp.where(i < n_i - 1, un_ref[0], 0.0)
    uext = jnp.concatenate([up, u, un], axis=0)
    edge = lax.broadcasted_iota(jnp.int32, (POOL_HALO, POOL_CH), 0)
    pos_first = i * ts + edge
    pos_last = pos_first + (ts - POOL_HALO)
    mix = jnp.dot(attn_ref[0], woa_ref[...], preferred_element_type=jnp.float32)
    for g, w in enumerate(POOL_WINDOWS):
        lo = w // 2
        hi = w - lo - 1
        ug = uext[:, g * POOL_CH:(g + 1) * POOL_CH]
        s = ug + pltpu.roll(ug, 1, axis=0)
        half = 1
        while 2 * half < w:
            s = pltpu.roll(s, half, axis=0) + pltpu.roll(s, n_ext - half, axis=0)
            half *= 2
        win = s[POOL_HALO:POOL_HALO + ts]
        cnt_first = (jnp.minimum(pos_first + hi + 1, seq_len) - jnp.maximum(pos_first - lo, 0)).astype(jnp.float32)
        cnt_last = (jnp.minimum(pos_last + hi + 1, seq_len) - jnp.maximum(pos_last - lo, 0)).astype(jnp.float32)
        mean = jnp.concatenate([win[:POOL_HALO] / cnt_first,
                                win[POOL_HALO:ts - POOL_HALO] * (1.0 / w),
                                win[ts - POOL_HALO:] / cnt_last], axis=0)
        d = mean - u[:, g * POOL_CH:(g + 1) * POOL_CH]
        mixed = jnp.dot(_bf16(d), wpool_ref[g], preferred_element_type=jnp.float32)
        pool_g = mixed * ps_ref[:, g * POOL_CH:(g + 1) * POOL_CH]
        mix = mix + jnp.dot(_bf16(pool_g), wop_ref[g * POOL_CH:(g + 1) * POOL_CH, :],
                            preferred_element_type=jnp.float32)

    h = x_ref[0] + mix
    h_ref[0] = h
    r = lax.rsqrt(jnp.mean(h * h, axis=-1, keepdims=True) + EPS)
    a2 = h * r * gffn_ref[...]
    a2_ref[...] = a2

    a_hi = _bf16(a2)
    a_lo = _bf16(a2 - a_hi.astype(jnp.float32))
    part = lax.dot_general(wr_ref[...], a_hi, _NT, preferred_element_type=jnp.float32)
    hi_lo = lax.dot_general(wr_ref[:ROUTER_ROWS], a_lo, _NT, preferred_element_type=jnp.float32)
    logits = part[:ROUTER_ROWS] + part[ROUTER_ROWS:] + hi_lo + br_ref[...]
    gl = logits[:N_GROUPS]
    ge = jnp.exp(gl - jnp.max(gl, axis=0, keepdims=True))
    gp = ge / jnp.sum(ge, axis=0, keepdims=True)
    g_val = jnp.max(gp, axis=0, keepdims=True)
    gi = lax.broadcasted_iota(jnp.int32, gp.shape, 0)
    g_idx = jnp.min(jnp.where(gp == g_val, gi, N_GROUPS), axis=0, keepdims=True)

    el = logits[EXPERT_ROW0:]
    ei = lax.broadcasted_iota(jnp.int32, el.shape, 0)
    in_grp = (ei // EXPERTS_PER_GROUP) == g_idx
    el_sel = jnp.where(in_grp, el, -jnp.inf)
    ex = jnp.exp(el_sel - jnp.max(el_sel, axis=0, keepdims=True))
    ep = ex / jnp.sum(ex, axis=0, keepdims=True)
    ep = jnp.where(in_grp, ep, -1.0)
    v1 = jnp.max(ep, axis=0, keepdims=True)
    i1 = jnp.min(jnp.where(ep == v1, ei, N_EXPERTS), axis=0, keepdims=True)
    ep2 = jnp.where(ei == i1, -1.0, ep)
    v2 = jnp.max(ep2, axis=0, keepdims=True)
    i2 = jnp.min(jnp.where(ep2 == v2, ei, N_EXPERTS), axis=0, keepdims=True)
    den = v1 + v2
    w1 = g_val * (v1 / den)
    w2 = g_val * (v2 / den)

    sel1 = ei == i1
    sel2 = ei == i2
    onehot = jnp.where(sel1, 1.0, jnp.where(sel2, 1.0, 0.0))
    before = jnp.dot(_bf16(onehot), tri_ref[...], preferred_element_type=jnp.float32)
    rank = cnt_scr[:, 0:1] + before.astype(jnp.int32)
    r1 = jnp.sum(jnp.where(sel1, rank, 0), axis=0, keepdims=True)
    r2 = jnp.sum(jnp.where(sel2, rank, 0), axis=0, keepdims=True)
    tot = jnp.sum(onehot, axis=1, keepdims=True).astype(jnp.int32)
    cnt_new = cnt_scr[...] + tot
    cnt_scr[...] = cnt_new
    cnt_ref[...] = cnt_new

    route_ref[0:1, :] = i1
    route_ref[1:2, :] = i2
    route_ref[2:3, :] = r1
    route_ref[3:4, :] = r2
    route_ref[4:5, :] = pltpu.bitcast(w1, jnp.int32)
    route_ref[5:6, :] = pltpu.bitcast(w2, jnp.int32)
    route_ref[6:8, :] = jnp.zeros((2, ts), jnp.int32)


def _post(x, attn, u, wpool, ps, woa, wop, gffn, wr, br, tri):
    B, S, D = x.shape
    ts = TS_POST
    n_i = S // ts
    T = B * S
    hb = ts // POOL_HALO
    full = lambda arr: pl.BlockSpec(arr.shape, lambda b, i: (0,) * arr.ndim)
    kern = functools.partial(_post_kernel, seq_len=S)
    return pl.pallas_call(
        kern,
        grid=(B, n_i),
        in_specs=[
            pl.BlockSpec((1, ts, D), lambda b, i: (b, i, 0)),
            pl.BlockSpec((1, ts, ATTN_WIDTH), lambda b, i: (b, i, 0)),
            pl.BlockSpec((1, ts, POOL_WIDTH), lambda b, i: (b, i, 0)),
            pl.BlockSpec((1, POOL_HALO, POOL_WIDTH), lambda b, i: (b, jnp.maximum(i * hb - 1, 0), 0)),
            pl.BlockSpec((1, POOL_HALO, POOL_WIDTH),
                         lambda b, i: (b, jnp.minimum((i + 1) * hb, S // POOL_HALO - 1), 0)),
            full(wpool), full(ps), full(woa), full(wop), full(gffn), full(wr), full(br), full(tri),
        ],
        out_specs=[
            pl.BlockSpec((1, ts, D), lambda b, i: (b, i, 0)),
            pl.BlockSpec((ts, D), lambda b, i: (b * n_i + i, 0)),
            pl.BlockSpec((SUBLANES, ts), lambda b, i: (0, b * n_i + i)),
            pl.BlockSpec((N_EXPERTS, LANES), lambda b, i: (0, 0)),
        ],
        out_shape=[
            jax.ShapeDtypeStruct((B, S, D), jnp.float32),
            jax.ShapeDtypeStruct((T, D), jnp.float32),
            jax.ShapeDtypeStruct((SUBLANES, T), jnp.int32),
            jax.ShapeDtypeStruct((N_EXPERTS, LANES), jnp.int32),
        ],
        scratch_shapes=[pltpu.VMEM((N_EXPERTS, LANES), jnp.int32)],
        compiler_params=pltpu.CompilerParams(
            dimension_semantics=("arbitrary", "arbitrary"), vmem_limit_bytes=VMEM_LIMIT),
        name="post",
    )(x, attn, u, u, u, wpool, ps, woa, wop, gffn, wr, br, tri)


def _row_copy(src, src_row, dst, dst_row, sem):
    return pltpu.make_async_copy(src.at[pl.ds(src_row, 1)], dst.at[pl.ds(dst_row, 1)], sem)


def _dispatch_kernel(dest_ref, a2_ref, xs_ref, sem, *, n_tok, tt):
    base = pl.program_id(0) * tt

    def body(c, carry):
        for j in range(ISSUE_UNROLL):
            t = c * ISSUE_UNROLL + j
            _row_copy(a2_ref, t, xs_ref, dest_ref[base + t], sem).start(priority=0)
            _row_copy(a2_ref, t, xs_ref, dest_ref[n_tok + base + t], sem).start(priority=1)
        return carry

    lax.fori_loop(0, tt // ISSUE_UNROLL, body, 0)
    whole = pltpu.make_async_copy(a2_ref, xs_ref.at[pl.ds(0, tt)], sem)
    whole.wait()
    whole.wait()


def _dispatch(dest, a2):
    n_tok = a2.shape[0]
    tt = TT_DISPATCH
    kern = functools.partial(_dispatch_kernel, n_tok=n_tok, tt=tt)
    return pl.pallas_call(
        kern,
        grid_spec=pltpu.PrefetchScalarGridSpec(
            num_scalar_prefetch=1,
            grid=(n_tok // tt,),
            in_specs=[pl.BlockSpec((tt, D_MODEL), lambda i, dest: (i, 0))],
            out_specs=pl.BlockSpec(memory_space=pl.ANY),
            scratch_shapes=[pltpu.SemaphoreType.DMA],
        ),
        out_shape=jax.ShapeDtypeStruct((2 * n_tok, D_MODEL), jnp.float32),
        compiler_params=pltpu.CompilerParams(
            dimension_semantics=("arbitrary",), has_side_effects=True),
        name="dispatch",
    )(dest, a2)


def _ffn_kernel(wt_ref, we_ref, wlo_ref, whi_ref, wfirst_ref, xs_ref, wg_ref, wu_ref, wd_ref, o_ref):
    w = pl.program_id(0)
    tm = xs_ref.shape[0]
    x = _bf16(xs_ref[...])
    g = jnp.dot(x, wg_ref[0], preferred_element_type=jnp.float32)
    up = jnp.dot(x, wu_ref[0], preferred_element_type=jnp.float32)
    hdn = g / (1.0 + jnp.exp(-g)) * up
    out = jnp.dot(_bf16(hdn), wd_ref[0], preferred_element_type=jnp.float32)
    row = lax.broadcasted_iota(jnp.int32, (tm, 1), 0)
    mine = jnp.logical_and(row >= wlo_ref[w], row < whi_ref[w])

    @pl.when(wfirst_ref[w] == 1)
    def _():
        o_ref[...] = jnp.where(mine, out, 0.0)

    @pl.when(wfirst_ref[w] == 0)
    def _():
        o_ref[...] = jnp.where(mine, out, o_ref[...])


def _ffn(plan, xs, wg, wu, wd):
    tm = TM_FFN
    n_items = plan[0].shape[0]
    tile = lambda w, wt, we, wlo, whi, wf: (wt[w], 0)
    expert = lambda w, wt, we, wlo, whi, wf: (we[w], 0, 0)
    return pl.pallas_call(
        _ffn_kernel,
        grid_spec=pltpu.PrefetchScalarGridSpec(
            num_scalar_prefetch=5,
            grid=(n_items,),
            in_specs=[
                pl.BlockSpec((tm, D_MODEL), tile),
                pl.BlockSpec((1, D_MODEL, EXPERT_HIDDEN), expert),
                pl.BlockSpec((1, D_MODEL, EXPERT_HIDDEN), expert),
                pl.BlockSpec((1, EXPERT_HIDDEN, D_MODEL), expert),
            ],
            out_specs=pl.BlockSpec((tm, D_MODEL), tile),
        ),
        out_shape=jax.ShapeDtypeStruct(xs.shape, jnp.float32),
        compiler_params=pltpu.CompilerParams(
            dimension_semantics=("arbitrary",), vmem_limit_bytes=VMEM_LIMIT),
        name="ffn",
    )(*plan, xs, wg, wu, wd)


def _combine_kernel(dest_ref, h_ref, gates_ref, os_ref, y_ref, buf, sems, *, n_tok, tc):
    i = pl.program_id(0)
    n = pl.num_programs(0)

    def issue(step, slot):
        base = step * tc

        def body(c, carry):
            for j in range(ISSUE_UNROLL):
                t = c * ISSUE_UNROLL + j
                _row_copy(os_ref, dest_ref[base + t], buf.at[slot], t, sems.at[slot]).start(priority=0)
                _row_copy(os_ref, dest_ref[n_tok + base + t], buf.at[slot], tc + t, sems.at[slot]).start(priority=1)
            return carry

        lax.fori_loop(0, tc // ISSUE_UNROLL, body, 0)

    @pl.when(i == 0)
    def _():
        issue(0, 0)

    @pl.when(i + 1 < n)
    def _():
        issue(i + 1, (i + 1) % 2)

    slot = i % 2
    pltpu.make_async_copy(os_ref.at[pl.ds(0, 2 * tc)], buf.at[slot], sems.at[slot]).wait()
    w1 = gates_ref[:, 0:1]
    w2 = gates_ref[:, 1:2]
    y_ref[...] = h_ref[...] + (w1 * buf[slot, :tc, :] + w2 * buf[slot, tc:, :])


def _combine(dest, h, gates, out_sorted):
    n_tok = h.shape[0]
    tc = TC_COMBINE
    kern = functools.partial(_combine_kernel, n_tok=n_tok, tc=tc)
    return pl.pallas_call(
        kern,
        grid_spec=pltpu.PrefetchScalarGridSpec(
            num_scalar_prefetch=1,
            grid=(n_tok // tc,),
            in_specs=[
                pl.BlockSpec((tc, D_MODEL), lambda i, dest: (i, 0)),
                pl.BlockSpec((tc, 2), lambda i, dest: (i, 0)),
                pl.BlockSpec(memory_space=pl.ANY),
            ],
            out_specs=pl.BlockSpec((tc, D_MODEL), lambda i, dest: (i, 0)),
            scratch_shapes=[
                pltpu.VMEM((2, 2 * tc, D_MODEL), jnp.float32),
                pltpu.SemaphoreType.DMA((2,)),
            ],
        ),
        out_shape=jax.ShapeDtypeStruct(h.shape, jnp.float32),
        compiler_params=pltpu.CompilerParams(
            dimension_semantics=("arbitrary",), vmem_limit_bytes=VMEM_LIMIT),
        name="combine",
    )(dest, h, gates, out_sorted)


def _rope_table(seq):
    inv_freq = 1.0 / (ROPE_THETA ** (jnp.arange(0, QK_ROPE_DIM, 2, dtype=jnp.float32) / QK_ROPE_DIM))
    ang = jnp.arange(seq, dtype=jnp.float32)[:, None] * inv_freq[None, :]
    emb = jnp.concatenate([ang, ang], axis=-1)
    return jnp.concatenate([jnp.cos(emb), jnp.sin(emb)], axis=-1)


def _prep_layer(w_in, g_q_lat, g_kv_lat, w_uq, w_ukv, g_q_head, g_k_head, w_pool, pool_scale, w_out,
                g_attn_norm, g_ffn_norm, w_group_router, b_group_router, w_expert_router, b_expert_router,
                w_gate, w_up, w_down):
    half = QK_ROPE_DIM // 2
    perm = jnp.concatenate([jnp.arange(half, QK_ROPE_DIM), jnp.arange(0, half)])
    sign = jnp.concatenate([-jnp.ones((half,), jnp.float32), jnp.ones((half,), jnp.float32)])
    o_kv = Q_LORA_RANK
    o_kr = o_kv + KV_LORA_RANK
    o_u = o_kr + QK_ROPE_DIM
    wkr = w_in[:, o_kr:o_u]
    win = _bf16(jnp.concatenate([w_in[:, :o_kr], wkr, wkr[:, perm], w_in[:, o_u:]], axis=1))
    wuq3 = w_uq.reshape(Q_LORA_RANK, MLA_HEADS, QK_DIM)
    rope_cols = wuq3[:, :, QK_NOPE_DIM:]
    wuq = _bf16(jnp.concatenate([wuq3, rope_cols[:, :, perm]], axis=2).reshape(Q_LORA_RANK, MLA_HEADS * HEAD_PAD))
    q_scale = (QK_DIM ** -0.5) * math.log2(math.e)
    gq_rope = g_q_head[QK_NOPE_DIM:]
    gq = (jnp.concatenate([g_q_head, gq_rope[perm] * sign]) * q_scale)[None, :]
    gk_rope = g_k_head[QK_NOPE_DIM:]
    gkn = g_k_head[None, :QK_NOPE_DIM]
    gkr = jnp.concatenate([gk_rope, gk_rope[perm] * sign])[None, :]
    wr = jnp.zeros((ROUTER_ROWS, D_MODEL), jnp.float32)
    wr = wr.at[:N_GROUPS].set(w_group_router.T).at[EXPERT_ROW0:].set(w_expert_router.T)
    wr_hi = _bf16(wr)
    wr_lo = _bf16(wr - wr_hi.astype(jnp.float32))
    wr = jnp.concatenate([wr_hi, wr_lo], axis=0)
    br = jnp.zeros((ROUTER_ROWS, 1), jnp.float32)
    br = br.at[:N_GROUPS, 0].set(b_group_router).at[EXPERT_ROW0:, 0].set(b_expert_router)
    return dict(
        gattn=g_attn_norm[None, :], win=win, gq_lat=g_q_lat[None, :], gkv_lat=g_kv_lat[None, :],
        wuq=wuq, wukv=_bf16(w_ukv), gq=gq, gkn=gkn, gkr=gkr,
        wpool=_bf16(w_pool), ps=pool_scale[None, :], woa=_bf16(w_out[:ATTN_WIDTH]), wop=_bf16(w_out[ATTN_WIDTH:]),
        gffn=g_ffn_norm[None, :], wr=wr, br=br,
        wg=_bf16(w_gate), wu=_bf16(w_up), wd=_bf16(w_down),
    )


def _route_plan(route, counts, tm):
    n_tok = route.shape[1]
    n_rows = 2 * n_tok
    n_tiles = n_rows // tm
    seg_end = jnp.cumsum(counts)
    seg_start = seg_end - counts
    experts = jnp.arange(N_EXPERTS, dtype=jnp.int32)[:, None, None]
    start_of = jnp.sum(jnp.where(route[None, 0:2] == experts, seg_start[:, None, None], 0), axis=0)
    dest = (start_of + route[2:4]).reshape(n_rows).astype(jnp.int32)
    first_tile = seg_start // tm
    n_items = jnp.where(counts > 0, (seg_end - 1) // tm - first_tile + 1, 0)
    item_end = jnp.cumsum(n_items)
    item_start = item_end - n_items
    w = jnp.arange(n_tiles + N_EXPERTS - 1, dtype=jnp.int32)
    used = w < item_end[-1]
    last_expert = jnp.max(jnp.where(counts > 0, jnp.arange(N_EXPERTS), 0))
    e_w = jnp.where(used, jnp.minimum(jnp.sum(w[:, None] >= item_end[None, :], axis=1), N_EXPERTS - 1),
                    last_expert).astype(jnp.int32)
    t_w = jnp.where(used, jnp.take(first_tile, e_w) + w - jnp.take(item_start, e_w), n_tiles - 1).astype(jnp.int32)
    lo_w = jnp.where(used, jnp.clip(jnp.take(seg_start, e_w) - t_w * tm, 0, tm), 0).astype(jnp.int32)
    hi_w = jnp.where(used, jnp.clip(jnp.take(seg_end, e_w) - t_w * tm, 0, tm), 0).astype(jnp.int32)
    first_w = jnp.concatenate([jnp.ones((1,), jnp.int32), (t_w[1:] != t_w[:-1]).astype(jnp.int32)])
    gates = lax.bitcast_convert_type(route[4:6], jnp.float32).T
    return dest, (t_w, e_w, lo_w, hi_w, first_w), gates


def _encoder_layer(x, p):
    B, S, D = x.shape
    tab = _rope_table(S)
    q, k, vt, u = _pre(x, tab, p["gattn"], p["win"], p["gq_lat"], p["gkv_lat"], p["wuq"], p["wukv"],
                       p["gq"], p["gkn"], p["gkr"])
    attn = _attention(q, k, vt)
    ts = TS_POST
    tri = _bf16(jnp.arange(ts)[:, None] < jnp.arange(ts)[None, :])
    h, a2, route, cnt = _post(x, attn, u, p["wpool"], p["ps"], p["woa"], p["wop"], p["gffn"],
                              p["wr"], p["br"], tri)
    dest, plan, gates = _route_plan(route, cnt[:, 0], TM_FFN)
    xs = _dispatch(dest, a2)
    out_sorted = _ffn(plan, xs, p["wg"], p["wu"], p["wd"])
    y = _combine(dest, h.reshape(B * S, D), gates, out_sorted)
    return y.reshape(B, S, D)


def kernel(x_prompt, x_sample, g_attn_norm, w_in, g_q_lat, g_kv_lat, w_uq, w_ukv, g_q_head, g_k_head, w_pool, pool_scale, w_out, g_ffn_norm, w_group_router, b_group_router, w_expert_router, b_expert_router, w_gate, w_up, w_down):
    depth = w_in.shape[0]
    layers = [
        _prep_layer(w_in[l], g_q_lat[l], g_kv_lat[l], w_uq[l], w_ukv[l], g_q_head[l], g_k_head[l], w_pool[l],
                    pool_scale[l], w_out[l], g_attn_norm[l], g_ffn_norm[l], w_group_router[l],
                    b_group_router[l], w_expert_router[l], b_expert_router[l], w_gate[l], w_up[l], w_down[l])
        for l in range(depth)
    ]
    outs = []
    for x in (x_prompt, x_sample):
        for p in layers:
            x = _encoder_layer(x, p)
        outs.append(x)
    return tuple(outs)
```

```python
import functools
import math

import jax
import jax.numpy as jnp
from jax import lax
from jax.experimental import pallas as pl
from jax.experimental.pallas import tpu as pltpu

D_MODEL = 1024
V_HEAD_DIM = 128
VT_ROWS = V_HEAD_DIM + 16
MLA_HEADS = 4
QK_NOPE_DIM = 128
QK_ROPE_DIM = 64
QK_DIM = QK_NOPE_DIM + QK_ROPE_DIM
Q_LORA_RANK = 384
KV_LORA_RANK = 256
ROPE_THETA = 10000.0
POOL_WINDOWS = (2, 4, 8, 16)
POOL_CH = 128
POOL_WIDTH = POOL_CH * len(POOL_WINDOWS)
ATTN_WIDTH = MLA_HEADS * V_HEAD_DIM
N_GROUPS = 4
EXPERTS_PER_GROUP = 8
N_EXPERTS = N_GROUPS * EXPERTS_PER_GROUP
EXPERT_HIDDEN = 256
EPS = 1e-6

LANES = 128
SUBLANES = 8
ROW_TILES = D_MODEL // LANES
HEAD_PAD = 256
POOL_HALO = 8
EXPERT_ROW0 = 16
ROUTER_ROWS = EXPERT_ROW0 + N_EXPERTS

VMEM_LIMIT = 52 * 1024 * 1024

TS_PRE = 512
TQ = 512
TK = 2048
TS_POST = 512
TT_DISPATCH = 512
TM_FFN = 512
TC_COMBINE = 512
ISSUE_UNROLL = 4

_NT = (((1,), (1,)), ((), ()))


def _bf16(x):
    return x.astype(jnp.bfloat16)


def _pre_kernel(x_ref, tab_ref, gattn_ref, win_ref, gq_lat_ref, gkv_lat_ref, wuq_ref, wukv_ref,
                gq_ref, gkn_ref, gkr_ref, q_ref, k_ref, vt_ref, u_ref):
    x = x_ref[0]
    r = lax.rsqrt(jnp.mean(x * x, axis=-1, keepdims=True) + EPS)
    a = _bf16(x * r * gattn_ref[...])
    p = jnp.dot(a, win_ref[...], preferred_element_type=jnp.float32)
    o_kv = Q_LORA_RANK
    o_kr = o_kv + KV_LORA_RANK
    o_u = o_kr + LANES
    c_q = p[:, :o_kv]
    c_kv = p[:, o_kv:o_kr]
    kr = p[:, o_kr:o_u]
    u_ref[0] = p[:, o_u:]

    rq_lat = lax.rsqrt(jnp.mean(c_q * c_q, axis=-1, keepdims=True) + EPS)
    cq = _bf16(c_q * rq_lat * gq_lat_ref[...])
    rkv_lat = lax.rsqrt(jnp.mean(c_kv * c_kv, axis=-1, keepdims=True) + EPS)
    ckv = _bf16(c_kv * rkv_lat * gkv_lat_ref[...])
    qf = jnp.dot(cq, wuq_ref[...], preferred_element_type=jnp.float32)
    kvf = jnp.dot(ckv, wukv_ref[...], preferred_element_type=jnp.float32)

    tab = tab_ref[...]
    lane = lax.broadcasted_iota(jnp.int32, tab.shape, 1)
    first_half = lane < QK_ROPE_DIM
    xk = kr * gkr_ref[...] * tab
    yk = xk + pltpu.roll(xk, QK_ROPE_DIM, axis=1)
    ssq_kr = jnp.sum(jnp.where(first_half, kr * kr, 0.0), axis=-1, keepdims=True)
    gq = gq_ref[...]
    inv_dim = 1.0 / QK_DIM
    for h in range(MLA_HEADS):
        qn = qf[:, h * HEAD_PAD:h * HEAD_PAD + LANES]
        qr = qf[:, h * HEAD_PAD + LANES:(h + 1) * HEAD_PAD]
        ssq = (jnp.sum(qn * qn, axis=-1, keepdims=True)
               + jnp.sum(jnp.where(first_half, qr * qr, 0.0), axis=-1, keepdims=True))
        rq = lax.rsqrt(ssq * inv_dim + EPS)
        q_ref[0, h, :, :LANES] = _bf16(qn * rq * gq[:, :LANES])
        q_ref[0, h, :, LANES:] = _bf16(qr * rq * gq[:, LANES:] * tab)
        kn = kvf[:, h * HEAD_PAD:h * HEAD_PAD + LANES]
        ssqk = jnp.sum(kn * kn, axis=-1, keepdims=True) + ssq_kr
        rk = lax.rsqrt(ssqk * inv_dim + EPS)
        k_ref[0, h, :, :LANES] = _bf16(kn * rk * gkn_ref[...])
        k_ref[0, h, :, LANES:] = _bf16(yk * rk)
        vt_ref[0, h, :V_HEAD_DIM, :] = _bf16(kvf[:, h * HEAD_PAD + LANES:(h + 1) * HEAD_PAD].T)
        vt_ref[0, h, V_HEAD_DIM:, :] = jnp.ones((VT_ROWS - V_HEAD_DIM, tab.shape[0]), jnp.bfloat16)


def _pre(x, tab, gattn, win, gq_lat, gkv_lat, wuq, wukv, gq, gkn, gkr):
    B, S, D = x.shape
    ts = TS_PRE
    full = lambda arr: pl.BlockSpec(arr.shape, lambda b, i: (0,) * arr.ndim)
    return pl.pallas_call(
        _pre_kernel,
        grid=(B, S // ts),
        in_specs=[
            pl.BlockSpec((1, ts, D), lambda b, i: (b, i, 0)),
            pl.BlockSpec((ts, LANES), lambda b, i: (i, 0)),
            full(gattn), full(win), full(gq_lat), full(gkv_lat), full(wuq), full(wukv),
            full(gq), full(gkn), full(gkr),
        ],
        out_specs=[
            pl.BlockSpec((1, MLA_HEADS, ts, HEAD_PAD), lambda b, i: (b, 0, i, 0)),
            pl.BlockSpec((1, MLA_HEADS, ts, HEAD_PAD), lambda b, i: (b, 0, i, 0)),
            pl.BlockSpec((1, MLA_HEADS, VT_ROWS, ts), lambda b, i: (b, 0, 0, i)),
            pl.BlockSpec((1, ts, POOL_WIDTH), lambda b, i: (b, i, 0)),
        ],
        out_shape=[
            jax.ShapeDtypeStruct((B, MLA_HEADS, S, HEAD_PAD), jnp.bfloat16),
            jax.ShapeDtypeStruct((B, MLA_HEADS, S, HEAD_PAD), jnp.bfloat16),
            jax.ShapeDtypeStruct((B, MLA_HEADS, VT_ROWS, S), jnp.bfloat16),
            jax.ShapeDtypeStruct((B, S, POOL_WIDTH), jnp.float32),
        ],
        compiler_params=pltpu.CompilerParams(
            dimension_semantics=("parallel", "parallel"), vmem_limit_bytes=VMEM_LIMIT),
        name="pre",
    )(x, tab, gattn, win, gq_lat, gkv_lat, wuq, wukv, gq, gkn, gkr)


def _attn_kernel(q0_ref, q1_ref, q2_ref, kc_ref, kn_ref, vt_ref, o_ref, st_a, st_b, m_a, m_b, acc_ref, *, tk):
    tq = q1_ref.shape[2]
    n_kv = st_a.shape[0] // tk

    def score_chunk(q_ref, k_ref, st_ref, c, mx):
        rows = pl.ds(pl.multiple_of(c * tk, tk), tk)
        st = lax.dot_general(k_ref[0, 0, rows, :], q_ref[0, 0], _NT,
                             preferred_element_type=jnp.float32)
        st_ref[rows, :] = st
        return jnp.maximum(mx, jnp.max(st, axis=0, keepdims=True))

    def value_chunk(st_ref, m_ref, c):
        rows = pl.ds(pl.multiple_of(c * tk, tk), tk)
        p = jnp.exp2(st_ref[rows, :] - m_ref[...])
        acc_ref[...] += jnp.dot(vt_ref[0, 0, :, rows], _bf16(p), preferred_element_type=jnp.float32)

    def phase(q_ref, k_ref, st_new, m_new, st_cur, m_cur):
        acc_ref[...] = jnp.zeros(acc_ref.shape, jnp.float32)

        def body(c, mx):
            mx = score_chunk(q_ref, k_ref, st_new, c, mx)
            value_chunk(st_cur, m_cur, c)
            return mx

        m_new[...] = lax.fori_loop(0, n_kv, body, jnp.full((1, tq), -jnp.inf, jnp.float32))
        acc = acc_ref[...]
        return _bf16((acc[:V_HEAD_DIM] / acc[V_HEAD_DIM:V_HEAD_DIM + 1]).T)

    @pl.when(pl.program_id(0) == 0)
    def _():
        m_a[...] = lax.fori_loop(0, n_kv, lambda c, mx: score_chunk(q0_ref, kc_ref, st_a, c, mx),
                                 jnp.full((1, tq), -jnp.inf, jnp.float32))

    o_ref[0, 0, :tq, :] = phase(q1_ref, kc_ref, st_b, m_b, st_a, m_a)
    o_ref[0, 0, tq:, :] = phase(q2_ref, kn_ref, st_a, m_a, st_b, m_b)


def _attention(q, k, vt):
    B, H, S, _ = q.shape
    tq = TQ
    pairs = S // (2 * tq)
    assert pairs * 2 * tq == S
    n_steps = B * H * pairs

    def decode(m):
        r = m % (H * pairs)
        return m // (H * pairs), r // pairs, r % pairs

    def following(m):
        return decode(jnp.minimum(m + 1, n_steps - 1))

    def q_tile(which):
        def index(m):
            b, h, j = following(m) if which == 2 else decode(m)
            return b, h, 2 * j + (which % 2), 0
        return pl.BlockSpec((1, 1, tq, HEAD_PAD), index)

    def head_block(rows, cols, step):
        def index(m):
            b, h, _ = step(m)
            return b, h, 0, 0
        return pl.BlockSpec((1, 1, rows, cols), index)

    def out_index(m):
        b, h, j = decode(m)
        return b, h, j, 0

    return pl.pallas_call(
        functools.partial(_attn_kernel, tk=min(TK, S)),
        grid=(n_steps,),
        in_specs=[
            q_tile(0), q_tile(1), q_tile(2),
            head_block(S, HEAD_PAD, decode), head_block(S, HEAD_PAD, following),
            head_block(VT_ROWS, S, decode),
        ],
        out_specs=pl.BlockSpec((1, 1, 2 * tq, V_HEAD_DIM), out_index),
        out_shape=jax.ShapeDtypeStruct((B, H, S, V_HEAD_DIM), jnp.bfloat16),
        scratch_shapes=[
            pltpu.VMEM((S, tq), jnp.float32),
            pltpu.VMEM((S, tq), jnp.float32),
            pltpu.VMEM((1, tq), jnp.float32),
            pltpu.VMEM((1, tq), jnp.float32),
            pltpu.VMEM((VT_ROWS, tq), jnp.float32),
        ],
        compiler_params=pltpu.CompilerParams(
            dimension_semantics=("arbitrary",), vmem_limit_bytes=VMEM_LIMIT),
        name="attn",
    )(q, q, q, k, k, vt)


def _post_kernel(x_ref, attn_ref, u_ref, up_ref, un_ref, wpool_ref, ps_ref, woa_ref, wop_ref,
                 gffn_ref, wr_ref, br_ref, tri_ref,
                 h_ref, a2_ref, route_ref, cnt_ref, cnt_scr, *, seq_len):
    b = pl.program_id(0)
    i = pl.program_id(1)
    n_i = pl.num_programs(1)
    ts = u_ref.shape[1]
    n_ext = ts + 2 * POOL_HALO

    @pl.when(jnp.logical_and(b == 0, i == 0))
    def _():
        cnt_scr[...] = jnp.zeros(cnt_scr.shape, jnp.int32)

    u = u_ref[0]
    up = jnp.where(i > 0, up_ref[0], 0.0)
    un = jnp.where(i < n_i - 1, un_ref[0], 0.0)
    uext = jnp.concatenate([up, u, un], axis=0)
    edge = lax.broadcasted_iota(jnp.int32, (POOL_HALO, POOL_CH), 0)
    pos_first = i * ts + edge
    pos_last = pos_first + (ts - POOL_HALO)
    attn = jnp.concatenate([attn_ref[0, hd] for hd in range(MLA_HEADS)], axis=1)
    mix = jnp.dot(attn, woa_ref[...], preferred_element_type=jnp.float32)
    for g, w in enumerate(POOL_WINDOWS):
        lo = w // 2
        hi = w - lo - 1
        ug = uext[:, g * POOL_CH:(g + 1) * POOL_CH]
        s = ug + pltpu.roll(ug, 1, axis=0)
        half = 1
        while 2 * half < w:
            s = pltpu.roll(s, half, axis=0) + pltpu.roll(s, n_ext - half, axis=0)
            half *= 2
        win = s[POOL_HALO:POOL_HALO + ts]
        cnt_first = (jnp.minimum(pos_first + hi + 1, seq_len) - jnp.maximum(pos_first - lo, 0)).astype(jnp.float32)
        cnt_last = (jnp.minimum(pos_last + hi + 1, seq_len) - jnp.maximum(pos_last - lo, 0)).astype(jnp.float32)
        mean = jnp.concatenate([win[:POOL_HALO] / cnt_first,
                                win[POOL_HALO:ts - POOL_HALO] * (1.0 / w),
                                win[ts - POOL_HALO:] / cnt_last], axis=0)
        d = mean - u[:, g * POOL_CH:(g + 1) * POOL_CH]
        mixed = jnp.dot(_bf16(d), wpool_ref[g], preferred_element_type=jnp.float32)
        pool_g = mixed * ps_ref[:, g * POOL_CH:(g + 1) * POOL_CH]
        mix = mix + jnp.dot(_bf16(pool_g), wop_ref[g * POOL_CH:(g + 1) * POOL_CH, :],
                            preferred_element_type=jnp.float32)

    h = x_ref[0] + mix
    h_ref[0] = h
    r = lax.rsqrt(jnp.mean(h * h, axis=-1, keepdims=True) + EPS)
    a2 = h * r * gffn_ref[...]
    for s8 in range(ROW_TILES):
        a2_ref[pl.ds(s8, ts, stride=SUBLANES), :] = a2[:, s8 * LANES:(s8 + 1) * LANES]

    a_hi = _bf16(a2)
    a_lo = _bf16(a2 - a_hi.astype(jnp.float32))
    part = lax.dot_general(wr_ref[...], a_hi, _NT, preferred_element_type=jnp.float32)
    hi_lo = lax.dot_general(wr_ref[:ROUTER_ROWS], a_lo, _NT, preferred_element_type=jnp.float32)
    logits = part[:ROUTER_ROWS] + part[ROUTER_ROWS:] + hi_lo + br_ref[...]
    gl = logits[:N_GROUPS]
    ge = jnp.exp(gl - jnp.max(gl, axis=0, keepdims=True))
    gp = ge / jnp.sum(ge, axis=0, keepdims=True)
    g_val = jnp.max(gp, axis=0, keepdims=True)
    gi = lax.broadcasted_iota(jnp.int32, gp.shape, 0)
    g_idx = jnp.min(jnp.where(gp == g_val, gi, N_GROUPS), axis=0, keepdims=True)

    el = logits[EXPERT_ROW0:]
    ei = lax.broadcasted_iota(jnp.int32, el.shape, 0)
    in_grp = (ei // EXPERTS_PER_GROUP) == g_idx
    el_sel = jnp.where(in_grp, el, -jnp.inf)
    ex = jnp.exp(el_sel - jnp.max(el_sel, axis=0, keepdims=True))
    ep = ex / jnp.sum(ex, axis=0, keepdims=True)
    ep = jnp.where(in_grp, ep, -1.0)
    v1 = jnp.max(ep, axis=0, keepdims=True)
    i1 = jnp.min(jnp.where(ep == v1, ei, N_EXPERTS), axis=0, keepdims=True)
    ep2 = jnp.where(ei == i1, -1.0, ep)
    v2 = jnp.max(ep2, axis=0, keepdims=True)
    i2 = jnp.min(jnp.where(ep2 == v2, ei, N_EXPERTS), axis=0, keepdims=True)
    den = v1 + v2
    w1 = g_val * (v1 / den)
    w2 = g_val * (v2 / den)

    sel1 = ei == i1
    sel2 = ei == i2
    onehot = jnp.where(sel1, 1.0, jnp.where(sel2, 1.0, 0.0))
    before = jnp.dot(_bf16(onehot), tri_ref[...], preferred_element_type=jnp.float32)
    rank = cnt_scr[:, 0:1] + before.astype(jnp.int32)
    r1 = jnp.sum(jnp.where(sel1, rank, 0), axis=0, keepdims=True)
    r2 = jnp.sum(jnp.where(sel2, rank, 0), axis=0, keepdims=True)
    tot = jnp.sum(onehot, axis=1, keepdims=True).astype(jnp.int32)
    cnt_new = cnt_scr[...] + tot
    cnt_scr[...] = cnt_new
    cnt_ref[...] = cnt_new

    route_ref[0:1, :] = i1
    route_ref[1:2, :] = i2
    route_ref[2:3, :] = r1
    route_ref[3:4, :] = r2
    route_ref[4:5, :] = pltpu.bitcast(w1, jnp.int32)
    route_ref[5:6, :] = pltpu.bitcast(w2, jnp.int32)
    route_ref[6:8, :] = jnp.zeros((2, ts), jnp.int32)


def _post(x, attn, u, wpool, ps, woa, wop, gffn, wr, br, tri):
    B, S, D = x.shape
    ts = TS_POST
    n_i = S // ts
    T = B * S
    hb = ts // POOL_HALO
    full = lambda arr: pl.BlockSpec(arr.shape, lambda b, i: (0,) * arr.ndim)
    kern = functools.partial(_post_kernel, seq_len=S)
    return pl.pallas_call(
        kern,
        grid=(B, n_i),
        in_specs=[
            pl.BlockSpec((1, ts, D), lambda b, i: (b, i, 0)),
            pl.BlockSpec((1, MLA_HEADS, ts, V_HEAD_DIM), lambda b, i: (b, 0, i, 0)),
            pl.BlockSpec((1, ts, POOL_WIDTH), lambda b, i: (b, i, 0)),
            pl.BlockSpec((1, POOL_HALO, POOL_WIDTH), lambda b, i: (b, jnp.maximum(i * hb - 1, 0), 0)),
            pl.BlockSpec((1, POOL_HALO, POOL_WIDTH),
                         lambda b, i: (b, jnp.minimum((i + 1) * hb, S // POOL_HALO - 1), 0)),
            full(wpool), full(ps), full(woa), full(wop), full(gffn), full(wr), full(br), full(tri),
        ],
        out_specs=[
            pl.BlockSpec((1, ts, D), lambda b, i: (b, i, 0)),
            pl.BlockSpec((ts * ROW_TILES, LANES), lambda b, i: (b * n_i + i, 0)),
            pl.BlockSpec((SUBLANES, ts), lambda b, i: (0, b * n_i + i)),
            pl.BlockSpec((N_EXPERTS, LANES), lambda b, i: (0, 0)),
        ],
        out_shape=[
            jax.ShapeDtypeStruct((B, S, D), jnp.float32),
            jax.ShapeDtypeStruct((T * ROW_TILES, LANES), jnp.float32),
            jax.ShapeDtypeStruct((SUBLANES, T), jnp.int32),
            jax.ShapeDtypeStruct((N_EXPERTS, LANES), jnp.int32),
        ],
        scratch_shapes=[pltpu.VMEM((N_EXPERTS, LANES), jnp.int32)],
        compiler_params=pltpu.CompilerParams(
            dimension_semantics=("arbitrary", "arbitrary"), vmem_limit_bytes=VMEM_LIMIT),
        name="post",
    )(x, attn, u, u, u, wpool, ps, woa, wop, gffn, wr, br, tri)


def _row_copy(src, src_row, dst, dst_row, sem):
    return pltpu.make_async_copy(
        src.at[pl.ds(pl.multiple_of(src_row * ROW_TILES, ROW_TILES), ROW_TILES)],
        dst.at[pl.ds(pl.multiple_of(dst_row * ROW_TILES, ROW_TILES), ROW_TILES)], sem)


def _dispatch_kernel(dest_ref, a2_ref, xs_ref, sem, *, n_tok, tt):
    base = pl.program_id(0) * tt

    def body(c, carry):
        for j in range(ISSUE_UNROLL):
            t = c * ISSUE_UNROLL + j
            _row_copy(a2_ref, t, xs_ref, dest_ref[base + t], sem).start(priority=0)
            _row_copy(a2_ref, t, xs_ref, dest_ref[n_tok + base + t], sem).start(priority=1)
        return carry

    lax.fori_loop(0, tt // ISSUE_UNROLL, body, 0)
    whole = pltpu.make_async_copy(a2_ref, xs_ref.at[pl.ds(0, tt * ROW_TILES)], sem)
    whole.wait()
    whole.wait()


def _dispatch(dest, a2):
    n_tok = a2.shape[0] // ROW_TILES
    tt = TT_DISPATCH
    kern = functools.partial(_dispatch_kernel, n_tok=n_tok, tt=tt)
    return pl.pallas_call(
        kern,
        grid_spec=pltpu.PrefetchScalarGridSpec(
            num_scalar_prefetch=1,
            grid=(n_tok // tt,),
            in_specs=[pl.BlockSpec((tt * ROW_TILES, LANES), lambda i, dest: (i, 0))],
            out_specs=pl.BlockSpec(memory_space=pl.ANY),
            scratch_shapes=[pltpu.SemaphoreType.DMA],
        ),
        out_shape=jax.ShapeDtypeStruct((2 * n_tok * ROW_TILES, LANES), jnp.float32),
        compiler_params=pltpu.CompilerParams(
            dimension_semantics=("arbitrary",), has_side_effects=True),
        name="dispatch",
    )(dest, a2)


def _ffn_kernel(wt_ref, we_ref, wlo_ref, whi_ref, wfirst_ref, xs_ref, wg_ref, wu_ref, wd_ref, o_ref):
    w = pl.program_id(0)
    tm = xs_ref.shape[0] // ROW_TILES
    x = _bf16(jnp.concatenate(
        [xs_ref[pl.ds(s8, tm, stride=SUBLANES), :] for s8 in range(ROW_TILES)], axis=1))
    g = jnp.dot(x, wg_ref[0], preferred_element_type=jnp.float32)
    up = jnp.dot(x, wu_ref[0], preferred_element_type=jnp.float32)
    hdn = g / (1.0 + jnp.exp(-g)) * up
    out = jnp.dot(_bf16(hdn), wd_ref[0], preferred_element_type=jnp.float32)
    row = lax.broadcasted_iota(jnp.int32, (tm, 1), 0)
    mine = jnp.logical_and(row >= wlo_ref[w], row < whi_ref[w])

    @pl.when(wfirst_ref[w] == 1)
    def _():
        for s8 in range(ROW_TILES):
            o_ref[pl.ds(s8, tm, stride=SUBLANES), :] = jnp.where(mine, out[:, s8 * LANES:(s8 + 1) * LANES], 0.0)

    @pl.when(wfirst_ref[w] == 0)
    def _():
        for s8 in range(ROW_TILES):
            rows = pl.ds(s8, tm, stride=SUBLANES)
            o_ref[rows, :] = jnp.where(mine, out[:, s8 * LANES:(s8 + 1) * LANES], o_ref[rows, :])


def _ffn(plan, xs, wg, wu, wd):
    tm = TM_FFN
    n_items = plan[0].shape[0]
    tile = lambda w, wt, we, wlo, whi, wf: (wt[w], 0)
    expert = lambda w, wt, we, wlo, whi, wf: (we[w], 0, 0)
    return pl.pallas_call(
        _ffn_kernel,
        grid_spec=pltpu.PrefetchScalarGridSpec(
            num_scalar_prefetch=5,
            grid=(n_items,),
            in_specs=[
                pl.BlockSpec((tm * ROW_TILES, LANES), tile),
                pl.BlockSpec((1, D_MODEL, EXPERT_HIDDEN), expert),
                pl.BlockSpec((1, D_MODEL, EXPERT_HIDDEN), expert),
                pl.BlockSpec((1, EXPERT_HIDDEN, D_MODEL), expert),
            ],
            out_specs=pl.BlockSpec((tm * ROW_TILES, LANES), tile),
        ),
        out_shape=jax.ShapeDtypeStruct(xs.shape, jnp.float32),
        compiler_params=pltpu.CompilerParams(
            dimension_semantics=("arbitrary",), vmem_limit_bytes=VMEM_LIMIT),
        name="ffn",
    )(*plan, xs, wg, wu, wd)


def _combine_kernel(dest_ref, h_ref, gates_ref, os_ref, y_ref, buf, sems, *, n_tok, tc):
    i = pl.program_id(0)
    n = pl.num_programs(0)

    def issue(step, slot):
        base = step * tc

        def body(c, carry):
            for j in range(ISSUE_UNROLL):
                t = c * ISSUE_UNROLL + j
                _row_copy(os_ref, dest_ref[base + t], buf.at[slot], t, sems.at[slot]).start(priority=0)
                _row_copy(os_ref, dest_ref[n_tok + base + t], buf.at[slot], tc + t, sems.at[slot]).start(priority=1)
            return carry

        lax.fori_loop(0, tc // ISSUE_UNROLL, body, 0)

    @pl.when(i == 0)
    def _():
        issue(0, 0)

    @pl.when(i + 1 < n)
    def _():
        issue(i + 1, (i + 1) % 2)

    slot = i % 2
    pltpu.make_async_copy(os_ref.at[pl.ds(0, 2 * tc * ROW_TILES)], buf.at[slot], sems.at[slot]).wait()
    w1 = gates_ref[:, 0:1]
    w2 = gates_ref[:, 1:2]
    for s8 in range(ROW_TILES):
        r1 = buf[slot, pl.ds(s8, tc, stride=SUBLANES), :]
        r2 = buf[slot, pl.ds(tc * ROW_TILES + s8, tc, stride=SUBLANES), :]
        cols = slice(s8 * LANES, (s8 + 1) * LANES)
        y_ref[:, cols] = h_ref[:, cols] + (w1 * r1 + w2 * r2)


def _combine(dest, h, gates, out_sorted):
    n_tok = h.shape[0]
    tc = TC_COMBINE
    kern = functools.partial(_combine_kernel, n_tok=n_tok, tc=tc)
    return pl.pallas_call(
        kern,
        grid_spec=pltpu.PrefetchScalarGridSpec(
            num_scalar_prefetch=1,
            grid=(n_tok // tc,),
            in_specs=[
                pl.BlockSpec((tc, D_MODEL), lambda i, dest: (i, 0)),
                pl.BlockSpec((tc, 2), lambda i, dest: (i, 0)),
                pl.BlockSpec(memory_space=pl.ANY),
            ],
            out_specs=pl.BlockSpec((tc, D_MODEL), lambda i, dest: (i, 0)),
            scratch_shapes=[
                pltpu.VMEM((2, 2 * tc * ROW_TILES, LANES), jnp.float32),
                pltpu.SemaphoreType.DMA((2,)),
            ],
        ),
        out_shape=jax.ShapeDtypeStruct(h.shape, jnp.float32),
        compiler_params=pltpu.CompilerParams(
            dimension_semantics=("arbitrary",), vmem_limit_bytes=VMEM_LIMIT),
        name="combine",
    )(dest, h, gates, out_sorted)


def _rope_table(seq):
    inv_freq = 1.0 / (ROPE_THETA ** (jnp.arange(0, QK_ROPE_DIM, 2, dtype=jnp.float32) / QK_ROPE_DIM))
    ang = jnp.arange(seq, dtype=jnp.float32)[:, None] * inv_freq[None, :]
    emb = jnp.concatenate([ang, ang], axis=-1)
    return jnp.concatenate([jnp.cos(emb), jnp.sin(emb)], axis=-1)


def _prep_layer(w_in, g_q_lat, g_kv_lat, w_uq, w_ukv, g_q_head, g_k_head, w_pool, pool_scale, w_out,
                g_attn_norm, g_ffn_norm, w_group_router, b_group_router, w_expert_router, b_expert_router,
                w_gate, w_up, w_down):
    half = QK_ROPE_DIM // 2
    perm = jnp.concatenate([jnp.arange(half, QK_ROPE_DIM), jnp.arange(0, half)])
    sign = jnp.concatenate([-jnp.ones((half,), jnp.float32), jnp.ones((half,), jnp.float32)])
    o_kv = Q_LORA_RANK
    o_kr = o_kv + KV_LORA_RANK
    o_u = o_kr + QK_ROPE_DIM
    wkr = w_in[:, o_kr:o_u]
    win = _bf16(jnp.concatenate([w_in[:, :o_kr], wkr, wkr[:, perm], w_in[:, o_u:]], axis=1))
    wuq3 = w_uq.reshape(Q_LORA_RANK, MLA_HEADS, QK_DIM)
    rope_cols = wuq3[:, :, QK_NOPE_DIM:]
    wuq = _bf16(jnp.concatenate([wuq3, rope_cols[:, :, perm]], axis=2).reshape(Q_LORA_RANK, MLA_HEADS * HEAD_PAD))
    q_scale = (QK_DIM ** -0.5) * math.log2(math.e)
    gq_rope = g_q_head[QK_NOPE_DIM:]
    gq = (jnp.concatenate([g_q_head, gq_rope[perm] * sign]) * q_scale)[None, :]
    gk_rope = g_k_head[QK_NOPE_DIM:]
    gkn = g_k_head[None, :QK_NOPE_DIM]
    gkr = jnp.concatenate([gk_rope, gk_rope[perm] * sign])[None, :]
    wr = jnp.zeros((ROUTER_ROWS, D_MODEL), jnp.float32)
    wr = wr.at[:N_GROUPS].set(w_group_router.T).at[EXPERT_ROW0:].set(w_expert_router.T)
    wr_hi = _bf16(wr)
    wr_lo = _bf16(wr - wr_hi.astype(jnp.float32))
    wr = jnp.concatenate([wr_hi, wr_lo], axis=0)
    br = jnp.zeros((ROUTER_ROWS, 1), jnp.float32)
    br = br.at[:N_GROUPS, 0].set(b_group_router).at[EXPERT_ROW0:, 0].set(b_expert_router)
    return dict(
        gattn=g_attn_norm[None, :], win=win, gq_lat=g_q_lat[None, :], gkv_lat=g_kv_lat[None, :],
        wuq=wuq, wukv=_bf16(w_ukv), gq=gq, gkn=gkn, gkr=gkr,
        wpool=_bf16(w_pool), ps=pool_scale[None, :], woa=_bf16(w_out[:ATTN_WIDTH]), wop=_bf16(w_out[ATTN_WIDTH:]),
        gffn=g_ffn_norm[None, :], wr=wr, br=br,
        wg=_bf16(w_gate), wu=_bf16(w_up), wd=_bf16(w_down),
    )


def _route_plan(route, counts, tm):
    n_tok = route.shape[1]
    n_rows = 2 * n_tok
    n_tiles = n_rows // tm
    seg_end = jnp.cumsum(counts)
    seg_start = seg_end - counts
    experts = jnp.arange(N_EXPERTS, dtype=jnp.int32)[:, None, None]
    start_of = jnp.sum(jnp.where(route[None, 0:2] == experts, seg_start[:, None, None], 0), axis=0)
    dest = (start_of + route[2:4]).reshape(n_rows).astype(jnp.int32)
    first_tile = seg_start // tm
    n_items = jnp.where(counts > 0, (seg_end - 1) // tm - first_tile + 1, 0)
    item_end = jnp.cumsum(n_items)
    item_start = item_end - n_items
    w = jnp.arange(n_tiles + N_EXPERTS - 1, dtype=jnp.int32)
    used = w < item_end[-1]
    last_expert = jnp.max(jnp.where(counts > 0, jnp.arange(N_EXPERTS), 0))
    e_w = jnp.where(used, jnp.minimum(jnp.sum(w[:, None] >= item_end[None, :], axis=1), N_EXPERTS - 1),
                    last_expert).astype(jnp.int32)
    t_w = jnp.where(used, jnp.take(first_tile, e_w) + w - jnp.take(item_start, e_w), n_tiles - 1).astype(jnp.int32)
    lo_w = jnp.where(used, jnp.clip(jnp.take(seg_start, e_w) - t_w * tm, 0, tm), 0).astype(jnp.int32)
    hi_w = jnp.where(used, jnp.clip(jnp.take(seg_end, e_w) - t_w * tm, 0, tm), 0).astype(jnp.int32)
    first_w = jnp.concatenate([jnp.ones((1,), jnp.int32), (t_w[1:] != t_w[:-1]).astype(jnp.int32)])
    gates = lax.bitcast_convert_type(route[4:6], jnp.float32).T
    return dest, (t_w, e_w, lo_w, hi_w, first_w), gates


def _encoder_layer(x, p):
    B, S, D = x.shape
    tab = _rope_table(S)
    q, k, vt, u = _pre(x, tab, p["gattn"], p["win"], p["gq_lat"], p["gkv_lat"], p["wuq"], p["wukv"],
                       p["gq"], p["gkn"], p["gkr"])
    attn = _attention(q, k, vt)
    ts = TS_POST
    tri = _bf16(jnp.arange(ts)[:, None] < jnp.arange(ts)[None, :])
    h, a2, route, cnt = _post(x, attn, u, p["wpool"], p["ps"], p["woa"], p["wop"], p["gffn"],
                              p["wr"], p["br"], tri)
    dest, plan, gates = _route_plan(route, cnt[:, 0], TM_FFN)
    xs = _dispatch(dest, a2)
    out_sorted = _ffn(plan, xs, p["wg"], p["wu"], p["wd"])
    y = _combine(dest, h.reshape(B * S, D), gates, out_sorted)
    return y.reshape(B, S, D)


def kernel(x_prompt, x_sample, g_attn_norm, w_in, g_q_lat, g_kv_lat, w_uq, w_ukv, g_q_head, g_k_head, w_pool, pool_scale, w_out, g_ffn_norm, w_group_router, b_group_router, w_expert_router, b_expert_router, w_gate, w_up, w_down):
    depth = w_in.shape[0]
    layers = [
        _prep_layer(w_in[l], g_q_lat[l], g_kv_lat[l], w_uq[l], w_ukv[l], g_q_head[l], g_k_head[l], w_pool[l],
                    pool_scale[l], w_out[l], g_attn_norm[l], g_ffn_norm[l], w_group_router[l],
                    b_group_router[l], w_expert_router[l], b_expert_router[l], w_gate[l], w_up[l], w_down[l])
        for l in range(depth)
    ]
    outs = []
    for x in (x_prompt, x_sample):
        for p in layers:
            x = _encoder_layer(x, p)
        outs.append(x)
    return tuple(outs)
```

```python
import functools

import jax
import jax.numpy as jnp
from jax import lax
from jax.experimental import pallas as pl
from jax.experimental.pallas import tpu as pltpu

D_MODEL = 1024
V_HEAD_DIM = 128
VT_ROWS = V_HEAD_DIM + 16
MLA_HEADS = 4
QK_NOPE_DIM = 128
QK_ROPE_DIM = 64
QK_DIM = QK_NOPE_DIM + QK_ROPE_DIM
Q_LORA_RANK = 384
KV_LORA_RANK = 256
ROPE_THETA = 10000.0
POOL_WINDOWS = (2, 4, 8, 16)
POOL_CH = 128
POOL_WIDTH = POOL_CH * len(POOL_WINDOWS)
ATTN_WIDTH = MLA_HEADS * V_HEAD_DIM
N_GROUPS = 4
EXPERTS_PER_GROUP = 8
N_EXPERTS = N_GROUPS * EXPERTS_PER_GROUP
TOP_K = 2
EXPERT_HIDDEN = 256
EPS = 1e-6

LANES = 128
SUBLANES = 8
ROW_TILES = D_MODEL // LANES
HEAD_PAD = 256
POOL_HALO = 8
EXPERT_ROW0 = 16
ROUTER_ROWS = EXPERT_ROW0 + N_EXPERTS

VMEM_LIMIT = 52 * 1024 * 1024

TS_PRE = 512
TQ = 512
TK = 2048
TS_POST = 512
TT_DISPATCH = 512
TM_FFN = 512
TC_COMBINE = 512
ISSUE_UNROLL = 4

_NT = (((1,), (1,)), ((), ()))


def _bf16(x):
    return x.astype(jnp.bfloat16)


def _to_slabs(x):
    return _bf16(x).reshape(x.shape[0], ROW_TILES, LANES)


def _from_slabs(s):
    return s.reshape(s.shape[0], D_MODEL)


def _pre_kernel(x_ref, tab_ref, gattn_ref, win_ref, gq_lat_ref, gkv_lat_ref, wuq_ref, wukv_ref,
                gq_ref, gkn_ref, gkr_ref, q_ref, k_ref, vt_ref, u_ref):
    x = x_ref[0]
    r = lax.rsqrt(jnp.mean(x * x, axis=-1, keepdims=True) + EPS)
    a = _bf16(x * r * gattn_ref[...])
    p = jnp.dot(a, win_ref[...], preferred_element_type=jnp.float32)
    o_kv = Q_LORA_RANK
    o_kr = o_kv + KV_LORA_RANK
    o_u = o_kr + LANES
    c_q = p[:, :o_kv]
    c_kv = p[:, o_kv:o_kr]
    kr = p[:, o_kr:o_u]
    u_ref[0] = p[:, o_u:]

    rq_lat = lax.rsqrt(jnp.mean(c_q * c_q, axis=-1, keepdims=True) + EPS)
    cq = _bf16(c_q * rq_lat * gq_lat_ref[...])
    rkv_lat = lax.rsqrt(jnp.mean(c_kv * c_kv, axis=-1, keepdims=True) + EPS)
    ckv = _bf16(c_kv * rkv_lat * gkv_lat_ref[...])
    qf = jnp.dot(cq, wuq_ref[...], preferred_element_type=jnp.float32)
    kvf = jnp.dot(ckv, wukv_ref[...], preferred_element_type=jnp.float32)

    tab = tab_ref[...]
    lane = lax.broadcasted_iota(jnp.int32, tab.shape, 1)
    first_half = lane < QK_ROPE_DIM
    xk = kr * gkr_ref[...] * tab
    yk = xk + pltpu.roll(xk, QK_ROPE_DIM, axis=1)
    ssq_kr = jnp.sum(jnp.where(first_half, kr * kr, 0.0), axis=-1, keepdims=True)
    gq = gq_ref[...]
    for h in range(MLA_HEADS):
        qn = qf[:, h * HEAD_PAD:h * HEAD_PAD + LANES]
        qr = qf[:, h * HEAD_PAD + LANES:(h + 1) * HEAD_PAD]
        ssq = (jnp.sum(qn * qn, axis=-1, keepdims=True)
               + jnp.sum(jnp.where(first_half, qr * qr, 0.0), axis=-1, keepdims=True))
        rq = lax.rsqrt(ssq / QK_DIM + EPS)
        q_ref[0, h, :, :LANES] = _bf16(qn * rq * gq[:, :LANES])
        q_ref[0, h, :, LANES:] = _bf16(qr * rq * gq[:, LANES:] * tab)
        kn = kvf[:, h * HEAD_PAD:h * HEAD_PAD + LANES]
        ssqk = jnp.sum(kn * kn, axis=-1, keepdims=True) + ssq_kr
        rk = lax.rsqrt(ssqk / QK_DIM + EPS)
        k_ref[0, h, :, :LANES] = _bf16(kn * rk * gkn_ref[...])
        k_ref[0, h, :, LANES:] = _bf16(yk * rk)
        vt_ref[0, h, :V_HEAD_DIM, :] = _bf16(kvf[:, h * HEAD_PAD + LANES:(h + 1) * HEAD_PAD].T)
        vt_ref[0, h, V_HEAD_DIM:, :] = jnp.ones((VT_ROWS - V_HEAD_DIM, tab.shape[0]), jnp.bfloat16)


def _pre(x, tab, gattn, win, gq_lat, gkv_lat, wuq, wukv, gq, gkn, gkr):
    B, S, D = x.shape
    ts = TS_PRE
    full = lambda arr: pl.BlockSpec(arr.shape, lambda b, i: (0,) * arr.ndim)
    return pl.pallas_call(
        _pre_kernel,
        grid=(B, S // ts),
        in_specs=[
            pl.BlockSpec((1, ts, D), lambda b, i: (b, i, 0)),
            pl.BlockSpec((ts, LANES), lambda b, i: (i, 0)),
            full(gattn), full(win), full(gq_lat), full(gkv_lat), full(wuq), full(wukv),
            full(gq), full(gkn), full(gkr),
        ],
        out_specs=[
            pl.BlockSpec((1, MLA_HEADS, ts, HEAD_PAD), lambda b, i: (b, 0, i, 0)),
            pl.BlockSpec((1, MLA_HEADS, ts, HEAD_PAD), lambda b, i: (b, 0, i, 0)),
            pl.BlockSpec((1, MLA_HEADS, VT_ROWS, ts), lambda b, i: (b, 0, 0, i)),
            pl.BlockSpec((1, ts, POOL_WIDTH), lambda b, i: (b, i, 0)),
        ],
        out_shape=[
            jax.ShapeDtypeStruct((B, MLA_HEADS, S, HEAD_PAD), jnp.bfloat16),
            jax.ShapeDtypeStruct((B, MLA_HEADS, S, HEAD_PAD), jnp.bfloat16),
            jax.ShapeDtypeStruct((B, MLA_HEADS, VT_ROWS, S), jnp.bfloat16),
            jax.ShapeDtypeStruct((B, S, POOL_WIDTH), jnp.float32),
        ],
        compiler_params=pltpu.CompilerParams(
            dimension_semantics=("parallel", "parallel"), vmem_limit_bytes=VMEM_LIMIT),
        name="pre",
    )(x, tab, gattn, win, gq_lat, gkv_lat, wuq, wukv, gq, gkn, gkr)


def _attn_kernel(q0_ref, q1_ref, q2_ref, kc_ref, kn_ref, vt_ref, o_ref, st_a, st_b, m_a, m_b, acc_ref, *, tk):
    tq = q1_ref.shape[2]
    n_kv = st_a.shape[0] // tk

    def score_chunk(q_ref, k_ref, st_ref, c, mx):
        rows = pl.ds(pl.multiple_of(c * tk, tk), tk)
        st = lax.dot_general(k_ref[0, 0, rows, :], q_ref[0, 0], _NT,
                             preferred_element_type=jnp.float32)
        st_ref[rows, :] = st
        return jnp.maximum(mx, jnp.max(st, axis=0, keepdims=True))

    def value_chunk(st_ref, m_ref, c):
        rows = pl.ds(pl.multiple_of(c * tk, tk), tk)
        p = jnp.exp(st_ref[rows, :] - m_ref[...])
        acc_ref[...] += jnp.dot(vt_ref[0, 0, :, rows], _bf16(p), preferred_element_type=jnp.float32)

    def phase(q_ref, k_ref, st_new, m_new, st_cur, m_cur):
        acc_ref[...] = jnp.zeros(acc_ref.shape, jnp.float32)

        def body(c, mx):
            mx = score_chunk(q_ref, k_ref, st_new, c, mx)
            value_chunk(st_cur, m_cur, c)
            return mx

        m_new[...] = lax.fori_loop(0, n_kv, body, jnp.full((1, tq), -jnp.inf, jnp.float32))
        acc = acc_ref[...]
        return _bf16((acc[:V_HEAD_DIM] / acc[V_HEAD_DIM:V_HEAD_DIM + 1]).T)

    @pl.when(pl.program_id(0) == 0)
    def _():
        m_a[...] = lax.fori_loop(0, n_kv, lambda c, mx: score_chunk(q0_ref, kc_ref, st_a, c, mx),
                                 jnp.full((1, tq), -jnp.inf, jnp.float32))

    o_ref[0, 0, :tq, :] = phase(q1_ref, kc_ref, st_b, m_b, st_a, m_a)
    o_ref[0, 0, tq:, :] = phase(q2_ref, kn_ref, st_a, m_a, st_b, m_b)


def _attention(q, k, vt):
    B, H, S, _ = q.shape
    tq = TQ
    pairs = S // (2 * tq)
    assert pairs * 2 * tq == S
    n_steps = B * H * pairs

    def decode(m):
        r = m % (H * pairs)
        return m // (H * pairs), r // pairs, r % pairs

    def following(m):
        return decode(jnp.minimum(m + 1, n_steps - 1))

    def q_tile(which):
        def index(m):
            b, h, j = following(m) if which == 2 else decode(m)
            return b, h, 2 * j + (which % 2), 0
        return pl.BlockSpec((1, 1, tq, HEAD_PAD), index)

    def head_block(rows, cols, step):
        def index(m):
            b, h, _ = step(m)
            return b, h, 0, 0
        return pl.BlockSpec((1, 1, rows, cols), index)

    def out_index(m):
        b, h, j = decode(m)
        return b, h, j, 0

    return pl.pallas_call(
        functools.partial(_attn_kernel, tk=min(TK, S)),
        grid=(n_steps,),
        in_specs=[
            q_tile(0), q_tile(1), q_tile(2),
            head_block(S, HEAD_PAD, decode), head_block(S, HEAD_PAD, following),
            head_block(VT_ROWS, S, decode),
        ],
        out_specs=pl.BlockSpec((1, 1, 2 * tq, V_HEAD_DIM), out_index),
        out_shape=jax.ShapeDtypeStruct((B, H, S, V_HEAD_DIM), jnp.bfloat16),
        scratch_shapes=[
            pltpu.VMEM((S, tq), jnp.float32),
            pltpu.VMEM((S, tq), jnp.float32),
            pltpu.VMEM((1, tq), jnp.float32),
            pltpu.VMEM((1, tq), jnp.float32),
            pltpu.VMEM((VT_ROWS, tq), jnp.float32),
        ],
        compiler_params=pltpu.CompilerParams(
            dimension_semantics=("arbitrary",), vmem_limit_bytes=VMEM_LIMIT),
        name="attn",
    )(q, q, q, k, k, vt)


def _post_kernel(x_ref, attn_ref, u_ref, up_ref, un_ref, wpool_ref, ps_ref, woa_ref, wop_ref,
                 gffn_ref, wr_ref, br_ref, tri_ref,
                 h_ref, a2_ref, route_ref, gate_ref, cnt_ref, cnt_scr, *, seq_len):
    b = pl.program_id(0)
    i = pl.program_id(1)
    n_i = pl.num_programs(1)
    ts = u_ref.shape[1]
    n_ext = ts + 2 * POOL_HALO

    @pl.when(jnp.logical_and(b == 0, i == 0))
    def _():
        cnt_scr[...] = jnp.zeros(cnt_scr.shape, jnp.int32)

    u = u_ref[0]
    up = jnp.where(i > 0, up_ref[0], 0.0)
    un = jnp.where(i < n_i - 1, un_ref[0], 0.0)
    uext = jnp.concatenate([up, u, un], axis=0)
    edge = lax.broadcasted_iota(jnp.int32, (POOL_HALO, POOL_CH), 0)
    pos_first = i * ts + edge
    pos_last = pos_first + (ts - POOL_HALO)
    attn = jnp.concatenate([attn_ref[0, hd] for hd in range(MLA_HEADS)], axis=1)
    mix = jnp.dot(attn, woa_ref[...], preferred_element_type=jnp.float32)
    for g, w in enumerate(POOL_WINDOWS):
        lo = w // 2
        hi = w - lo - 1
        ug = uext[:, g * POOL_CH:(g + 1) * POOL_CH]
        s = ug + pltpu.roll(ug, 1, axis=0)
        half = 1
        while 2 * half < w:
            s = pltpu.roll(s, half, axis=0) + pltpu.roll(s, n_ext - half, axis=0)
            half *= 2
        win = s[POOL_HALO:POOL_HALO + ts]
        cnt_first = (jnp.minimum(pos_first + hi + 1, seq_len) - jnp.maximum(pos_first - lo, 0)).astype(jnp.float32)
        cnt_last = (jnp.minimum(pos_last + hi + 1, seq_len) - jnp.maximum(pos_last - lo, 0)).astype(jnp.float32)
        mean = jnp.concatenate([win[:POOL_HALO] / cnt_first,
                                win[POOL_HALO:ts - POOL_HALO] * (1.0 / w),
                                win[ts - POOL_HALO:] / cnt_last], axis=0)
        d = mean - u[:, g * POOL_CH:(g + 1) * POOL_CH]
        mixed = jnp.dot(_bf16(d), wpool_ref[g], preferred_element_type=jnp.float32)
        pool_g = mixed * ps_ref[:, g * POOL_CH:(g + 1) * POOL_CH]
        mix = mix + jnp.dot(_bf16(pool_g), wop_ref[g * POOL_CH:(g + 1) * POOL_CH, :],
                            preferred_element_type=jnp.float32)

    h = x_ref[0] + mix
    h_ref[0] = h
    r = lax.rsqrt(jnp.mean(h * h, axis=-1, keepdims=True) + EPS)
    a2 = h * r * gffn_ref[...]
    a2_ref[...] = _to_slabs(a2)

    a_hi = _bf16(a2)
    a_lo = _bf16(a2 - a_hi.astype(jnp.float32))
    part = lax.dot_general(wr_ref[...], a_hi, _NT, preferred_element_type=jnp.float32)
    hi_lo = lax.dot_general(wr_ref[:ROUTER_ROWS], a_lo, _NT, preferred_element_type=jnp.float32)
    logits = part[:ROUTER_ROWS] + part[ROUTER_ROWS:] + hi_lo + br_ref[...]
    gl = logits[:N_GROUPS]
    ge = jnp.exp(gl - jnp.max(gl, axis=0, keepdims=True))
    gp = ge / jnp.sum(ge, axis=0, keepdims=True)
    g_val = jnp.max(gp, axis=0, keepdims=True)
    gi = lax.broadcasted_iota(jnp.int32, gp.shape, 0)
    g_idx = jnp.min(jnp.where(gp == g_val, gi, N_GROUPS), axis=0, keepdims=True)

    el = logits[EXPERT_ROW0:]
    ei = lax.broadcasted_iota(jnp.int32, el.shape, 0)
    in_grp = (ei // EXPERTS_PER_GROUP) == g_idx
    el_sel = jnp.where(in_grp, el, -jnp.inf)
    ex = jnp.exp(el_sel - jnp.max(el_sel, axis=0, keepdims=True))
    ep = ex / jnp.sum(ex, axis=0, keepdims=True)
    ep = jnp.where(in_grp, ep, -1.0)
    v1 = jnp.max(ep, axis=0, keepdims=True)
    i1 = jnp.min(jnp.where(ep == v1, ei, N_EXPERTS), axis=0, keepdims=True)
    ep2 = jnp.where(ei == i1, -1.0, ep)
    v2 = jnp.max(ep2, axis=0, keepdims=True)
    i2 = jnp.min(jnp.where(ep2 == v2, ei, N_EXPERTS), axis=0, keepdims=True)
    den = v1 + v2
    w1 = g_val * (v1 / den)
    w2 = g_val * (v2 / den)

    sel1 = ei == i1
    sel2 = ei == i2
    onehot = jnp.where(sel1, 1.0, jnp.where(sel2, 1.0, 0.0))
    before = jnp.dot(_bf16(onehot), tri_ref[...], preferred_element_type=jnp.float32)
    rank = cnt_scr[:, 0:1] + before.astype(jnp.int32)
    r1 = jnp.sum(jnp.where(sel1, rank, 0), axis=0, keepdims=True)
    r2 = jnp.sum(jnp.where(sel2, rank, 0), axis=0, keepdims=True)
    tot = jnp.sum(onehot, axis=1, keepdims=True).astype(jnp.int32)
    cnt_new = cnt_scr[...] + tot
    cnt_scr[...] = cnt_new
    cnt_ref[...] = cnt_new

    route_ref[0:1, :] = i1
    route_ref[1:2, :] = i2
    route_ref[2:3, :] = r1
    route_ref[3:4, :] = r2
    route_ref[4:8, :] = jnp.zeros((4, ts), jnp.int32)
    gate_ref[0:1, :] = w1
    gate_ref[1:2, :] = w2


def _post(x, attn, u, wpool, ps, woa, wop, gffn, wr, br, tri):
    B, S, D = x.shape
    ts = TS_POST
    n_i = S // ts
    T = B * S
    hb = ts // POOL_HALO
    full = lambda arr: pl.BlockSpec(arr.shape, lambda b, i: (0,) * arr.ndim)
    kern = functools.partial(_post_kernel, seq_len=S)
    return pl.pallas_call(
        kern,
        grid=(B, n_i),
        in_specs=[
            pl.BlockSpec((1, ts, D), lambda b, i: (b, i, 0)),
            pl.BlockSpec((1, MLA_HEADS, ts, V_HEAD_DIM), lambda b, i: (b, 0, i, 0)),
            pl.BlockSpec((1, ts, POOL_WIDTH), lambda b, i: (b, i, 0)),
            pl.BlockSpec((1, POOL_HALO, POOL_WIDTH), lambda b, i: (b, jnp.maximum(i * hb - 1, 0), 0)),
            pl.BlockSpec((1, POOL_HALO, POOL_WIDTH),
                         lambda b, i: (b, jnp.minimum((i + 1) * hb, S // POOL_HALO - 1), 0)),
            full(wpool), full(ps), full(woa), full(wop), full(gffn), full(wr), full(br), full(tri),
        ],
        out_specs=[
            pl.BlockSpec((1, ts, D), lambda b, i: (b, i, 0)),
            pl.BlockSpec((ts, ROW_TILES, LANES), lambda b, i: (b * n_i + i, 0, 0)),
            pl.BlockSpec((SUBLANES, ts), lambda b, i: (0, b * n_i + i)),
            pl.BlockSpec((TOP_K, ts), lambda b, i: (0, b * n_i + i)),
            pl.BlockSpec((N_EXPERTS, LANES), lambda b, i: (0, 0)),
        ],
        out_shape=[
            jax.ShapeDtypeStruct((B, S, D), jnp.float32),
            jax.ShapeDtypeStruct((T, ROW_TILES, LANES), jnp.bfloat16),
            jax.ShapeDtypeStruct((SUBLANES, T), jnp.int32),
            jax.ShapeDtypeStruct((TOP_K, T), jnp.float32),
            jax.ShapeDtypeStruct((N_EXPERTS, LANES), jnp.int32),
        ],
        scratch_shapes=[pltpu.VMEM((N_EXPERTS, LANES), jnp.int32)],
        compiler_params=pltpu.CompilerParams(
            dimension_semantics=("arbitrary", "arbitrary"), vmem_limit_bytes=VMEM_LIMIT),
        name="post",
    )(x, attn, u, u, u, wpool, ps, woa, wop, gffn, wr, br, tri)


def _row_copy(src, src_row, dst, dst_row, sem):
    return pltpu.make_async_copy(src.at[src_row], dst.at[dst_row], sem)


def _dispatch_kernel(dest_ref, a2_ref, xs_ref, sem, *, n_tok, tt):
    base = pl.program_id(0) * tt

    def body(c, carry):
        for j in range(ISSUE_UNROLL):
            t = c * ISSUE_UNROLL + j
            _row_copy(a2_ref, t, xs_ref, dest_ref[base + t], sem).start(priority=0)
            _row_copy(a2_ref, t, xs_ref, dest_ref[n_tok + base + t], sem).start(priority=1)
        return carry

    lax.fori_loop(0, tt // ISSUE_UNROLL, body, 0)
    whole = pltpu.make_async_copy(a2_ref, xs_ref.at[pl.ds(0, tt)], sem)
    whole.wait()
    whole.wait()


def _dispatch(dest, a2):
    n_tok = a2.shape[0]
    tt = TT_DISPATCH
    kern = functools.partial(_dispatch_kernel, n_tok=n_tok, tt=tt)
    return pl.pallas_call(
        kern,
        grid_spec=pltpu.PrefetchScalarGridSpec(
            num_scalar_prefetch=1,
            grid=(n_tok // tt,),
            in_specs=[pl.BlockSpec((tt, ROW_TILES, LANES), lambda i, dest: (i, 0, 0))],
            out_specs=pl.BlockSpec(memory_space=pl.ANY),
            scratch_shapes=[pltpu.SemaphoreType.DMA],
        ),
        out_shape=jax.ShapeDtypeStruct((2 * n_tok, ROW_TILES, LANES), jnp.bfloat16),
        compiler_params=pltpu.CompilerParams(
            dimension_semantics=("arbitrary",), has_side_effects=True),
        name="dispatch",
    )(dest, a2)


def _ffn_kernel(wt_ref, we_ref, wlo_ref, whi_ref, wfirst_ref, xs_ref, wg_ref, wu_ref, wd_ref, o_ref):
    w = pl.program_id(0)
    tm = xs_ref.shape[0]
    x = _from_slabs(xs_ref[...])
    g = jnp.dot(x, wg_ref[0], preferred_element_type=jnp.float32)
    up = jnp.dot(x, wu_ref[0], preferred_element_type=jnp.float32)
    hdn = g / (1.0 + jnp.exp(-g)) * up
    out = jnp.dot(_bf16(hdn), wd_ref[0], preferred_element_type=jnp.float32)
    row = lax.broadcasted_iota(jnp.int32, (tm, 1), 0)
    mine = jnp.logical_and(row >= wlo_ref[w], row < whi_ref[w])

    @pl.when(wfirst_ref[w] == 1)
    def _():
        o_ref[...] = _to_slabs(jnp.where(mine, out, 0.0))

    @pl.when(wfirst_ref[w] == 0)
    def _():
        o_ref[...] = _to_slabs(jnp.where(mine, out, _from_slabs(o_ref[...]).astype(jnp.float32)))


def _ffn(plan, xs, wg, wu, wd):
    tm = TM_FFN
    n_items = plan[0].shape[0]
    tile = lambda w, wt, we, wlo, whi, wf: (wt[w], 0, 0)
    expert = lambda w, wt, we, wlo, whi, wf: (we[w], 0, 0)
    return pl.pallas_call(
        _ffn_kernel,
        grid_spec=pltpu.PrefetchScalarGridSpec(
            num_scalar_prefetch=5,
            grid=(n_items,),
            in_specs=[
                pl.BlockSpec((tm, ROW_TILES, LANES), tile),
                pl.BlockSpec((1, D_MODEL, EXPERT_HIDDEN), expert),
                pl.BlockSpec((1, D_MODEL, EXPERT_HIDDEN), expert),
                pl.BlockSpec((1, EXPERT_HIDDEN, D_MODEL), expert),
            ],
            out_specs=pl.BlockSpec((tm, ROW_TILES, LANES), tile),
        ),
        out_shape=jax.ShapeDtypeStruct(xs.shape, jnp.bfloat16),
        compiler_params=pltpu.CompilerParams(
            dimension_semantics=("arbitrary",), vmem_limit_bytes=VMEM_LIMIT),
        name="ffn",
    )(*plan, xs, wg, wu, wd)


def _combine_kernel(dest_ref, h_ref, gates_ref, os_ref, y_ref, buf, sems, *, n_tok, tc):
    i = pl.program_id(0)
    n = pl.num_programs(0)

    def issue(step, slot):
        base = step * tc

        def body(c, carry):
            for j in range(ISSUE_UNROLL):
                t = c * ISSUE_UNROLL + j
                _row_copy(os_ref, dest_ref[base + t], buf.at[slot], t, sems.at[slot]).start(priority=0)
                _row_copy(os_ref, dest_ref[n_tok + base + t], buf.at[slot], tc + t, sems.at[slot]).start(priority=1)
            return carry

        lax.fori_loop(0, tc // ISSUE_UNROLL, body, 0)

    @pl.when(i == 0)
    def _():
        issue(0, 0)

    @pl.when(i + 1 < n)
    def _():
        issue(i + 1, (i + 1) % 2)

    slot = i % 2
    pltpu.make_async_copy(os_ref.at[pl.ds(0, 2 * tc)], buf.at[slot], sems.at[slot]).wait()
    w1 = gates_ref[:, 0:1]
    w2 = gates_ref[:, 1:2]
    r1 = _from_slabs(buf[slot, :tc]).astype(jnp.float32)
    r2 = _from_slabs(buf[slot, tc:]).astype(jnp.float32)
    y_ref[...] = h_ref[...] + (w1 * r1 + w2 * r2)


def _combine(dest, h, gates, out_sorted):
    n_tok = h.shape[0]
    tc = TC_COMBINE
    kern = functools.partial(_combine_kernel, n_tok=n_tok, tc=tc)
    return pl.pallas_call(
        kern,
        grid_spec=pltpu.PrefetchScalarGridSpec(
            num_scalar_prefetch=1,
            grid=(n_tok // tc,),
            in_specs=[
                pl.BlockSpec((tc, D_MODEL), lambda i, dest: (i, 0)),
                pl.BlockSpec((tc, 2), lambda i, dest: (i, 0)),
                pl.BlockSpec(memory_space=pl.ANY),
            ],
            out_specs=pl.BlockSpec((tc, D_MODEL), lambda i, dest: (i, 0)),
            scratch_shapes=[
                pltpu.VMEM((2, 2 * tc, ROW_TILES, LANES), jnp.bfloat16),
                pltpu.SemaphoreType.DMA((2,)),
            ],
        ),
        out_shape=jax.ShapeDtypeStruct(h.shape, jnp.float32),
        compiler_params=pltpu.CompilerParams(
            dimension_semantics=("arbitrary",), vmem_limit_bytes=VMEM_LIMIT),
        name="combine",
    )(dest, h, gates, out_sorted)


def _rope_table(seq):
    inv_freq = 1.0 / (ROPE_THETA ** (jnp.arange(0, QK_ROPE_DIM, 2, dtype=jnp.float32) / QK_ROPE_DIM))
    ang = jnp.arange(seq, dtype=jnp.float32)[:, None] * inv_freq[None, :]
    emb = jnp.concatenate([ang, ang], axis=-1)
    return jnp.concatenate([jnp.cos(emb), jnp.sin(emb)], axis=-1)


def _prep_layer(w_in, g_q_lat, g_kv_lat, w_uq, w_ukv, g_q_head, g_k_head, w_pool, pool_scale, w_out,
                g_attn_norm, g_ffn_norm, w_group_router, b_group_router, w_expert_router, b_expert_router,
                w_gate, w_up, w_down):
    half = QK_ROPE_DIM // 2
    perm = jnp.concatenate([jnp.arange(half, QK_ROPE_DIM), jnp.arange(0, half)])
    sign = jnp.concatenate([-jnp.ones((half,), jnp.float32), jnp.ones((half,), jnp.float32)])
    o_kv = Q_LORA_RANK
    o_kr = o_kv + KV_LORA_RANK
    o_u = o_kr + QK_ROPE_DIM
    wkr = w_in[:, o_kr:o_u]
    win = _bf16(jnp.concatenate([w_in[:, :o_kr], wkr, wkr[:, perm], w_in[:, o_u:]], axis=1))
    wuq3 = w_uq.reshape(Q_LORA_RANK, MLA_HEADS, QK_DIM)
    rope_cols = wuq3[:, :, QK_NOPE_DIM:]
    wuq = _bf16(jnp.concatenate([wuq3, rope_cols[:, :, perm]], axis=2).reshape(Q_LORA_RANK, MLA_HEADS * HEAD_PAD))
    q_scale = QK_DIM ** -0.5
    gq_rope = g_q_head[QK_NOPE_DIM:]
    gq = (jnp.concatenate([g_q_head, gq_rope[perm] * sign]) * q_scale)[None, :]
    gk_rope = g_k_head[QK_NOPE_DIM:]
    gkn = g_k_head[None, :QK_NOPE_DIM]
    gkr = jnp.concatenate([gk_rope, gk_rope[perm] * sign])[None, :]
    wr = jnp.zeros((ROUTER_ROWS, D_MODEL), jnp.float32)
    wr = wr.at[:N_GROUPS].set(w_group_router.T).at[EXPERT_ROW0:].set(w_expert_router.T)
    wr_hi = _bf16(wr)
    wr_lo = _bf16(wr - wr_hi.astype(jnp.float32))
    wr = jnp.concatenate([wr_hi, wr_lo], axis=0)
    br = jnp.zeros((ROUTER_ROWS, 1), jnp.float32)
    br = br.at[:N_GROUPS, 0].set(b_group_router).at[EXPERT_ROW0:, 0].set(b_expert_router)
    return dict(
        gattn=g_attn_norm[None, :], win=win, gq_lat=g_q_lat[None, :], gkv_lat=g_kv_lat[None, :],
        wuq=wuq, wukv=_bf16(w_ukv), gq=gq, gkn=gkn, gkr=gkr,
        wpool=_bf16(w_pool), ps=pool_scale[None, :], woa=_bf16(w_out[:ATTN_WIDTH]), wop=_bf16(w_out[ATTN_WIDTH:]),
        gffn=g_ffn_norm[None, :], wr=wr, br=br,
        wg=_bf16(w_gate), wu=_bf16(w_up), wd=_bf16(w_down),
    )


def _route_plan(route, counts, tm):
    n_tok = route.shape[1]
    n_rows = 2 * n_tok
    n_tiles = n_rows // tm
    seg_end = jnp.cumsum(counts)
    seg_start = seg_end - counts
    experts = jnp.arange(N_EXPERTS, dtype=jnp.int32)[:, None, None]
    start_of = jnp.sum(jnp.where(route[None, 0:2] == experts, seg_start[:, None, None], 0), axis=0)
    dest = (start_of + route[2:4]).reshape(n_rows).astype(jnp.int32)
    first_tile = seg_start // tm
    n_items = jnp.where(counts > 0, (seg_end - 1) // tm - first_tile + 1, 0)
    item_end = jnp.cumsum(n_items)
    item_start = item_end - n_items
    w = jnp.arange(n_tiles + N_EXPERTS - 1, dtype=jnp.int32)
    used = w < item_end[-1]
    last_expert = jnp.max(jnp.where(counts > 0, jnp.arange(N_EXPERTS), 0))
    e_w = jnp.where(used, jnp.minimum(jnp.sum(w[:, None] >= item_end[None, :], axis=1), N_EXPERTS - 1),
                    last_expert).astype(jnp.int32)
    t_w = jnp.where(used, jnp.take(first_tile, e_w) + w - jnp.take(item_start, e_w), n_tiles - 1).astype(jnp.int32)
    lo_w = jnp.where(used, jnp.clip(jnp.take(seg_start, e_w) - t_w * tm, 0, tm), 0).astype(jnp.int32)
    hi_w = jnp.where(used, jnp.clip(jnp.take(seg_end, e_w) - t_w * tm, 0, tm), 0).astype(jnp.int32)
    first_w = jnp.concatenate([jnp.ones((1,), jnp.int32), (t_w[1:] != t_w[:-1]).astype(jnp.int32)])
    return dest, (t_w, e_w, lo_w, hi_w, first_w)


def _encoder_layer(x, p):
    B, S, D = x.shape
    tab = _rope_table(S)
    q, k, vt, u = _pre(x, tab, p["gattn"], p["win"], p["gq_lat"], p["gkv_lat"], p["wuq"], p["wukv"],
                       p["gq"], p["gkn"], p["gkr"])
    attn = _attention(q, k, vt)
    ts = TS_POST
    tri = _bf16(jnp.arange(ts)[:, None] < jnp.arange(ts)[None, :])
    h, a2, route, gates, cnt = _post(x, attn, u, p["wpool"], p["ps"], p["woa"], p["wop"], p["gffn"],
                                     p["wr"], p["br"], tri)
    dest, plan = _route_plan(route, cnt[:, 0], TM_FFN)
    xs = _dispatch(dest, a2)
    out_sorted = _ffn(plan, xs, p["wg"], p["wu"], p["wd"])
    y = _combine(dest, h.reshape(B * S, D), gates.T, out_sorted)
    return y.reshape(B, S, D)


def kernel(x_prompt, x_sample, g_attn_norm, w_in, g_q_lat, g_kv_lat, w_uq, w_ukv, g_q_head, g_k_head, w_pool, pool_scale, w_out, g_ffn_norm, w_group_router, b_group_router, w_expert_router, b_expert_router, w_gate, w_up, w_down):
    depth = w_in.shape[0]
    layers = [
        _prep_layer(w_in[l], g_q_lat[l], g_kv_lat[l], w_uq[l], w_ukv[l], g_q_head[l], g_k_head[l], w_pool[l],
                    pool_scale[l], w_out[l], g_attn_norm[l], g_ffn_norm[l], w_group_router[l],
                    b_group_router[l], w_expert_router[l], b_expert_router[l], w_gate[l], w_up[l], w_down[l])
        for l in range(depth)
    ]
    outs = []
    for x in (x_prompt, x_sample):
        for p in layers:
            x = _encoder_layer(x, p)
        outs.append(x)
    return tuple(outs)
```
